```python
import jax, jax.numpy as jnp
from jax import lax
import numpy as np

D_MODEL = 1024
BATCH = 4
SEQ = 4096
DEPTH = 1

CHUNK = 64
Q_BLOCK = 2 * CHUNK
HEAD_DIM = 64
N_FOX_HEADS = 8
N_SB_HEADS = 8
D_FOX = N_FOX_HEADS * HEAD_DIM
D_SB = N_SB_HEADS * HEAD_DIM
D_MIX = D_FOX + D_SB
D_IN = 3 * D_FOX + 3 * D_SB + N_FOX_HEADS
N_EXPERTS = 256
TOP_K = 8
N_GROUPS = 8
TOP_GROUPS = 4
D_EXPERT = 256
D_SHARED = 256
ROUTED_SCALE = 2.5
MOE_BLOCK = 128
DEEPNORM_ALPHA = (2.0 * DEPTH) ** 0.25
DEEPNORM_BETA = (8.0 * DEPTH) ** -0.25
LN_EPS = 1e-5
RMS_EPS = 1e-6

kernel_name = "hybrid_fox_stickbreak_moe_block"


def layer_norm(x, g, b):
    xf = x.astype(jnp.float32)
    mu = jnp.mean(xf, axis=-1, keepdims=True)
    var = jnp.mean(jnp.square(xf - mu), axis=-1, keepdims=True)
    return ((xf - mu) * lax.rsqrt(var + LN_EPS) * g + b).astype(x.dtype)


def head_rms_norm(o, g):
    h = o.shape[1]
    of = o.astype(jnp.float32)
    of = of * lax.rsqrt(jnp.mean(jnp.square(of), axis=-1, keepdims=True) + RMS_EPS)
    return (of * g.reshape(h, HEAD_DIM)[None, :, None, :]).astype(o.dtype)


def forgetting_attention(q, k, v, log_f):
    seq = q.shape[2]
    scale = HEAD_DIM ** -0.5
    F = jnp.cumsum(log_f, axis=-1)
    outs = []
    for i in range(seq // Q_BLOCK):
        q0, q1 = i * Q_BLOCK, (i + 1) * Q_BLOCK
        kp, vp = k[:, :, :q1], v[:, :, :q1]
        s = jnp.einsum("bhqd,bhkd->bhqk", q[:, :, q0:q1], kp).astype(jnp.float32) * scale
        s = s + F[:, :, q0:q1, None] - F[:, :, None, :q1]
        t_idx = jnp.arange(q0, q1)[:, None]
        s_idx = jnp.arange(q1)[None, :]
        s = jnp.where(s_idx <= t_idx, s, -jnp.inf)
        p = jax.nn.softmax(s, axis=-1)
        outs.append(jnp.einsum("bhqk,bhkd->bhqd", p.astype(v.dtype), vp))
    return jnp.concatenate(outs, axis=2)


def stick_breaking_attention(q, k, v):
    seq = q.shape[2]
    scale = HEAD_DIM ** -0.5
    outs = []
    for i in range(seq // Q_BLOCK):
        q0, q1 = i * Q_BLOCK, (i + 1) * Q_BLOCK
        kp, vp = k[:, :, :q1], v[:, :, :q1]
        z = jnp.einsum("bhqd,bhkd->bhqk", q[:, :, q0:q1], kp).astype(jnp.float32) * scale
        t_idx = jnp.arange(q0, q1)[:, None]
        s_idx = jnp.arange(q1)[None, :]
        past = s_idx < t_idx
        log_beta = jax.nn.log_sigmoid(z)
        log_keep = jnp.where(past, jax.nn.log_sigmoid(-z), 0.0)
        rem = lax.cumsum(log_keep, axis=3, reverse=True) - log_keep
        a = jnp.where(past, jnp.exp(log_beta + rem), 0.0)
        outs.append(jnp.einsum("bhqk,bhkd->bhqd", a.astype(v.dtype), vp))
    return jnp.concatenate(outs, axis=2)


def mixer_sublayer(u, w_in, b_f, fox_norm_g, sb_norm_g, w_out):
    b, s, _ = u.shape
    proj = jnp.einsum("bsd,de->bse", u, w_in)
    q_f, k_f, v_f, q_s, k_s, v_s, f_logit = jnp.split(
        proj,
        [D_FOX, 2 * D_FOX, 3 * D_FOX, 3 * D_FOX + D_SB, 3 * D_FOX + 2 * D_SB, 3 * D_FOX + 3 * D_SB],
        axis=-1)

    def heads(t, h):
        return t.reshape(b, s, h, HEAD_DIM).transpose(0, 2, 1, 3)

    log_f = jax.nn.log_sigmoid((f_logit + b_f).astype(jnp.float32)).transpose(0, 2, 1)
    o_f = forgetting_attention(heads(q_f, N_FOX_HEADS), heads(k_f, N_FOX_HEADS),
                               heads(v_f, N_FOX_HEADS), log_f)
    o_s = stick_breaking_attention(heads(q_s, N_SB_HEADS), heads(k_s, N_SB_HEADS),
                                   heads(v_s, N_SB_HEADS))
    o = jnp.concatenate([head_rms_norm(o_f, fox_norm_g), head_rms_norm(o_s, sb_norm_g)], axis=1)
    o = o.transpose(0, 2, 1, 3).reshape(b, s, D_MIX)
    return jnp.einsum("bse,ed->bsd", o, w_out)


def moe_sublayer(u, w_router, router_bias, w_gate_e, w_up_e, w_down_e,
                 w_gate_sh, w_up_sh, w_down_sh):
    b, s, d = u.shape
    n = b * s
    xf = u.reshape(n, d)
    scores = jax.nn.sigmoid(jnp.einsum("nd,de->ne", xf, w_router).astype(jnp.float32))
    sel = scores + router_bias.astype(jnp.float32)
    per_group = N_EXPERTS // N_GROUPS
    grp_score = jnp.sum(lax.top_k(sel.reshape(n, N_GROUPS, per_group), 2)[0], axis=-1)
    _, top_g = lax.top_k(grp_score, TOP_GROUPS)
    g_mask = jnp.any(top_g[:, :, None] == jnp.arange(N_GROUPS)[None, None, :], axis=1)
    sel = jnp.where(jnp.repeat(g_mask, per_group, axis=1), sel, -jnp.inf)
    _, top_e = lax.top_k(sel, TOP_K)
    w = jnp.take_along_axis(scores, top_e, axis=1)
    w = w / jnp.sum(w, axis=-1, keepdims=True) * ROUTED_SCALE

    nk = n * TOP_K
    e_flat = top_e.reshape(nk)
    tok_flat = jnp.arange(nk, dtype=jnp.int32) // TOP_K
    w_flat = w.reshape(nk)
    order = jnp.argsort(e_flat)
    e_sorted, tok_sorted, w_sorted = e_flat[order], tok_flat[order], w_flat[order]
    counts = jnp.zeros((N_EXPERTS,), jnp.int32).at[e_flat].add(1)
    padded = (counts + MOE_BLOCK - 1) // MOE_BLOCK * MOE_BLOCK
    start = jnp.cumsum(counts) - counts
    pstart = jnp.cumsum(padded) - padded
    dest = pstart[e_sorted] + (jnp.arange(nk, dtype=jnp.int32) - start[e_sorted])
    n_blocks = -(-nk // MOE_BLOCK) + N_EXPERTS
    cap = n_blocks * MOE_BLOCK
    buf_tok = jnp.full((cap,), n, jnp.int32).at[dest].set(tok_sorted)
    buf_w = jnp.zeros((cap,), jnp.float32).at[dest].set(w_sorted)
    blk_start = jnp.arange(n_blocks, dtype=jnp.int32) * MOE_BLOCK
    blk_expert = jnp.clip(jnp.searchsorted(pstart + padded, blk_start, side="right"),
                          0, N_EXPERTS - 1).astype(jnp.int32)
    x_pad = jnp.concatenate([xf, jnp.zeros((1, d), xf.dtype)], axis=0)

    def expert_block(args):
        toks, wts, e = args
        xb = x_pad[toks]
        h = jax.nn.silu(xb @ w_gate_e[e]) * (xb @ w_up_e[e])
        return (h @ w_down_e[e]) * wts[:, None].astype(xb.dtype)

    y = lax.map(expert_block, (buf_tok.reshape(n_blocks, MOE_BLOCK),
                               buf_w.reshape(n_blocks, MOE_BLOCK), blk_expert))
    routed = jnp.zeros((n + 1, d), y.dtype).at[buf_tok].add(y.reshape(cap, d))[:n]
    shared = (jax.nn.silu(xf @ w_gate_sh) * (xf @ w_up_sh)) @ w_down_sh
    return (routed + shared).reshape(b, s, d)


def setup_inputs(seed: int = 0) -> dict:
    key = jax.random.key(seed)
    ks = jax.random.split(key, 22)
    nrm = jax.random.normal
    col = jnp.arange(D_IN)
    is_v = (((col >= 2 * D_FOX) & (col < 3 * D_FOX))
            | ((col >= 3 * D_FOX + 2 * D_SB) & (col < 3 * D_FOX + 3 * D_SB)))
    col_scale = jnp.where(is_v, DEEPNORM_BETA, 1.0).astype(jnp.float32)
    return {
        "x": nrm(ks[0], (BATCH, SEQ, D_MODEL), jnp.float32),
        "c": nrm(ks[1], (BATCH, D_MODEL), jnp.float32),
        "w_ada": nrm(ks[2], (DEPTH, D_MODEL, 6 * D_MODEL), jnp.float32) * (0.3 * D_MODEL ** -0.5),
        "b_ada": 0.02 * nrm(ks[3], (DEPTH, 6 * D_MODEL), jnp.float32),
        "w_in": nrm(ks[4], (DEPTH, D_MODEL, D_IN), jnp.float32) * (D_MODEL ** -0.5) * col_scale,
        "b_f": jax.random.uniform(ks[5], (DEPTH, N_FOX_HEADS), jnp.float32, 1.0, 4.0),
        "fox_norm_g": 1.0 + 0.02 * nrm(ks[6], (DEPTH, D_FOX), jnp.float32),
        "sb_norm_g": 1.0 + 0.02 * nrm(ks[7], (DEPTH, D_SB), jnp.float32),
        "w_out": nrm(ks[8], (DEPTH, D_MIX, D_MODEL), jnp.float32) * (D_MIX ** -0.5 * DEEPNORM_BETA),
        "ln1_g": 1.0 + 0.02 * nrm(ks[9], (DEPTH, D_MODEL), jnp.float32),
        "ln1_b": 0.02 * nrm(ks[10], (DEPTH, D_MODEL), jnp.float32),
        "w_router": nrm(ks[11], (DEPTH, D_MODEL, N_EXPERTS), jnp.float32) * D_MODEL ** -0.5,
        "router_bias": 0.01 * nrm(ks[12], (DEPTH, N_EXPERTS), jnp.float32),
        "w_gate_e": nrm(ks[13], (DEPTH, N_EXPERTS, D_MODEL, D_EXPERT), jnp.float32) * (D_MODEL ** -0.5 * DEEPNORM_BETA),
        "w_up_e": nrm(ks[14], (DEPTH, N_EXPERTS, D_MODEL, D_EXPERT), jnp.float32) * (D_MODEL ** -0.5 * DEEPNORM_BETA),
        "w_down_e": nrm(ks[15], (DEPTH, N_EXPERTS, D_EXPERT, D_MODEL), jnp.float32) * (D_EXPERT ** -0.5 * DEEPNORM_BETA),
        "w_gate_sh": nrm(ks[16], (DEPTH, D_MODEL, D_SHARED), jnp.float32) * (D_MODEL ** -0.5 * DEEPNORM_BETA),
        "w_up_sh": nrm(ks[17], (DEPTH, D_MODEL, D_SHARED), jnp.float32) * (D_MODEL ** -0.5 * DEEPNORM_BETA),
        "w_down_sh": nrm(ks[18], (DEPTH, D_SHARED, D_MODEL), jnp.float32) * (D_SHARED ** -0.5 * DEEPNORM_BETA),
        "ln2_g": 1.0 + 0.02 * nrm(ks[19], (DEPTH, D_MODEL), jnp.float32),
        "ln2_b": 0.02 * nrm(ks[20], (DEPTH, D_MODEL), jnp.float32),
    }


def reference(x, c, w_ada, b_ada, w_in, b_f, fox_norm_g, sb_norm_g, w_out, ln1_g, ln1_b,
              w_router, router_bias, w_gate_e, w_up_e, w_down_e,
              w_gate_sh, w_up_sh, w_down_sh, ln2_g, ln2_b):
    for l in range(DEPTH):
        ada = jnp.einsum("bd,de->be", jax.nn.silu(c), w_ada[l]) + b_ada[l]
        shift1, scale1, gate1, shift2, scale2, gate2 = [t[:, None, :] for t in jnp.split(ada, 6, axis=-1)]
        u = x * (1.0 + scale1) + shift1
        mix = mixer_sublayer(u, w_in[l], b_f[l], fox_norm_g[l], sb_norm_g[l], w_out[l])
        x = layer_norm(DEEPNORM_ALPHA * x + (1.0 + gate1) * mix, ln1_g[l], ln1_b[l])
        u = x * (1.0 + scale2) + shift2
        ffn = moe_sublayer(u, w_router[l], router_bias[l], w_gate_e[l], w_up_e[l], w_down_e[l],
                           w_gate_sh[l], w_up_sh[l], w_down_sh[l])
        x = layer_norm(DEEPNORM_ALPHA * x + (1.0 + gate2) * ffn, ln2_g[l], ln2_b[l])
    return x
```

```python
import functools

import jax
import jax.numpy as jnp
from jax import lax
from jax.experimental import pallas as pl
from jax.experimental.pallas import tpu as pltpu

F32 = jnp.float32
BF16 = jnp.bfloat16
I32 = jnp.int32
U32 = jnp.uint32

HEAD_DIM = 64
N_HEADS = 8
SLAB = 128
N_EXPERTS = 256
TOP_K = 8
N_GROUPS = 8
TOP_GROUPS = 4
GROUP_SIZE = N_EXPERTS // N_GROUPS
ROUTED_SCALE = 2.5
MOE_BLOCK = 128
LN_EPS = 1e-5
RMS_EPS = 1e-6
NEG_BIG = -1e30

VMEM_LIMIT = 56 * 1024 * 1024


def _cparams(sem):
    return pltpu.CompilerParams(dimension_semantics=sem, vmem_limit_bytes=VMEM_LIMIT)


def _dot(a, b):
    return jnp.dot(a, b, preferred_element_type=F32)


def _dot_nt(a, b):
    return lax.dot_general(a, b, (((1,), (1,)), ((), ())), preferred_element_type=F32)


def _dot_tn(a, b):
    return lax.dot_general(a, b, (((0,), (0,)), ((), ())), preferred_element_type=F32)


def _split2(x):
    hi = x.astype(BF16)
    lo = (x - hi.astype(F32)).astype(BF16)
    return hi, lo


def _split3(x):
    p1 = x.astype(BF16)
    r1 = x - p1.astype(F32)
    p2 = r1.astype(BF16)
    p3 = (r1 - p2.astype(F32)).astype(BF16)
    return p1, p2, p3


def _dot3(a, b):
    a_hi, a_lo = _split2(a)
    b_hi, b_lo = _split2(b)
    return _dot(a_hi, b_hi) + _dot(a_lo, b_hi) + _dot(a_hi, b_lo)


def _dot3_nt(a, b):
    a_hi, a_lo = _split2(a)
    b_hi, b_lo = _split2(b)
    return _dot_nt(a_hi, b_hi) + _dot_nt(a_lo, b_hi) + _dot_nt(a_hi, b_lo)


def _softplus(z):
    return jnp.maximum(z, 0.0) + jnp.log(1.0 + jnp.exp(-jnp.abs(z)))


def _layer_norm(v, g, b):
    mu = jnp.mean(v, axis=-1, keepdims=True)
    d = v - mu
    var = jnp.mean(d * d, axis=-1, keepdims=True)
    return d * lax.rsqrt(var + LN_EPS) * g + b


def _ada_kernel(c_ref, w_ref, b_ref, o_ref):
    c = c_ref[...]
    s = c * jax.nn.sigmoid(c)
    o_ref[...] = _dot3(s, w_ref[...]) + b_ref[...]


def _ada(c_pad, w, b):
    rows, d = c_pad.shape
    n = w.shape[1]
    tn = 1024
    return pl.pallas_call(
        _ada_kernel,
        out_shape=jax.ShapeDtypeStruct((rows, n), F32),
        grid=(n // tn,),
        in_specs=[pl.BlockSpec((rows, d), lambda j: (0, 0)),
                  pl.BlockSpec((d, tn), lambda j: (0, j)),
                  pl.BlockSpec((1, tn), lambda j: (0, j))],
        out_specs=pl.BlockSpec((rows, tn), lambda j: (0, j)),
        compiler_params=_cparams(("arbitrary",)),
        name="ada",
    )(c_pad, w, b)


def _inproj_kernel(x_ref, sc_ref, sh_ref, wtok_ref, wfeat_ref, bf_ref, place_ref,
                   kf_ref, ks_ref, qf_ref, qs_ref, vf_ref, vs_ref, carry_ref, *, t, tk):
    d_slab = N_HEADS * SLAB
    d_val = N_HEADS * HEAD_DIM

    @pl.when(pl.program_id(1) == 0)
    def _():
        carry_ref[...] = jnp.zeros_like(carry_ref)

    u = x_ref[0] * (1.0 + sc_ref[0]) + sh_ref[0]
    ub = u.astype(BF16)

    flog = _dot(ub, wtok_ref[:, 2 * d_slab:]) + bf_ref[...]
    logf = -_softplus(-flog)
    lane = lax.broadcasted_iota(I32, logf.shape, 1)
    logf = jnp.where(lane < N_HEADS, logf, 0.0)
    row = lax.broadcasted_iota(I32, (t, t), 0)
    col = lax.broadcasted_iota(I32, (t, t), 1)
    tri = jnp.where(row >= col, 1.0, 0.0).astype(BF16)
    lf_hi, lf_lo = _split2(logf)
    cum = _dot(tri, lf_hi) + _dot(tri, lf_lo) + carry_ref[...]
    carry_ref[...] = cum[t - 1:t, :]
    n1, n2, n3 = _split3(-cum)
    extras = _dot(n1, place_ref[0]) + _dot(n2, place_ref[1]) + _dot(n3, place_ref[2])

    kf_ref[0] = (_dot(ub, wtok_ref[:, :d_slab]) + extras).astype(BF16)
    ks_ref[0] = _dot(ub, wtok_ref[:, d_slab:2 * d_slab]).astype(BF16)

    qf = _dot_nt(wfeat_ref[:d_slab, :], ub)
    r = lax.broadcasted_iota(I32, qf.shape, 0) % SLAB
    qf = jnp.where((r >= HEAD_DIM) & (r < HEAD_DIM + 3), 1.0, qf)
    qf_ref[0] = qf.astype(BF16)
    qs_ref[0] = _dot_nt(wfeat_ref[d_slab:2 * d_slab, :], ub).astype(BF16)
    vf = _dot_nt(wfeat_ref[2 * d_slab:2 * d_slab + d_val, :], ub).astype(BF16)
    vs = _dot_nt(wfeat_ref[2 * d_slab + d_val:, :], ub).astype(BF16)
    for c in range(t // tk):
        vf_ref[0, c] = vf[:, c * tk:(c + 1) * tk]
        vs_ref[0, c] = vs[:, c * tk:(c + 1) * tk]


def _inproj(x, sc, sh, wtok, wfeat, bf_row, place, *, t, tk):
    b, s, d = x.shape
    d_slab = N_HEADS * SLAB
    d_val = N_HEADS * HEAD_DIM
    nt = s // t
    out_shape = (
        jax.ShapeDtypeStruct((b, s, d_slab), BF16),
        jax.ShapeDtypeStruct((b, s, d_slab), BF16),
        jax.ShapeDtypeStruct((b, d_slab, s), BF16),
        jax.ShapeDtypeStruct((b, d_slab, s), BF16),
        jax.ShapeDtypeStruct((b, s // tk, d_val, tk), BF16),
        jax.ShapeDtypeStruct((b, s // tk, d_val, tk), BF16),
    )
    tok_spec = pl.BlockSpec((1, t, d_slab), lambda i, j: (i, j, 0))
    feat_spec = pl.BlockSpec((1, d_slab, t), lambda i, j: (i, 0, j))
    val_spec = pl.BlockSpec((1, t // tk, d_val, tk), lambda i, j: (i, j, 0, 0))
    return pl.pallas_call(
        functools.partial(_inproj_kernel, t=t, tk=tk),
        out_shape=out_shape,
        grid=(b, nt),
        in_specs=[pl.BlockSpec((1, t, d), lambda i, j: (i, j, 0)),
                  pl.BlockSpec((1, 1, d), lambda i, j: (i, 0, 0)),
                  pl.BlockSpec((1, 1, d), lambda i, j: (i, 0, 0)),
                  pl.BlockSpec(wtok.shape, lambda i, j: (0, 0)),
                  pl.BlockSpec(wfeat.shape, lambda i, j: (0, 0)),
                  pl.BlockSpec((1, SLAB), lambda i, j: (0, 0)),
                  pl.BlockSpec(place.shape, lambda i, j: (0, 0, 0))],
        out_specs=(tok_spec, tok_spec, feat_spec, feat_spec, val_spec, val_spec),
        scratch_shapes=[pltpu.VMEM((1, SLAB), F32)],
        compiler_params=_cparams(("arbitrary", "arbitrary")),
        name="inproj",
    )(x, sc, sh, wtok, wfeat, bf_row, place)


def _head_norm_store(acc, g_ref, o_ref):
    ms = jnp.mean(acc * acc, axis=0, keepdims=True)
    o_ref[0] = (acc * lax.rsqrt(ms + RMS_EPS) * g_ref[...]).astype(BF16)


def _fox_kernel(q_ref, k_ref, v_ref, g_ref, o_ref, *, tq):
    qi = pl.program_id(2)
    qt = q_ref[0]

    def tile(kb, carry, masked):
        m, l, acc = carry
        start = pl.multiple_of(kb * tq, tq)
        k = k_ref[0, pl.ds(start, tq), :]
        s = _dot(k, qt)
        if masked:
            kid = lax.broadcasted_iota(I32, s.shape, 0)
            qid = lax.broadcasted_iota(I32, s.shape, 1)
            s = jnp.where(kid <= qid, s, NEG_BIG)
        m_new = jnp.maximum(m, jnp.max(s, axis=0, keepdims=True))
        p = jnp.exp(s - m_new)
        alpha = jnp.exp(m - m_new)
        l = alpha * l + jnp.sum(p, axis=0, keepdims=True)
        acc = alpha * acc + _dot(v_ref[0, kb], p.astype(BF16))
        return m_new, l, acc

    init = (jnp.full((1, tq), NEG_BIG, F32), jnp.zeros((1, tq), F32),
            jnp.zeros((HEAD_DIM, tq), F32))
    carry = lax.fori_loop(0, qi, lambda kb, c: tile(kb, c, False), init)
    _, l, acc = tile(qi, carry, True)
    _head_norm_store(acc / l, g_ref, o_ref)


def _sb_kernel(q_ref, k_ref, v_ref, g_ref, o_ref, *, tq):
    qi = pl.program_id(2)
    qt = q_ref[0]
    row = lax.broadcasted_iota(I32, (tq, tq), 0)
    col = lax.broadcasted_iota(I32, (tq, tq), 1)
    upper = jnp.where(col > row, 1.0, 0.0).astype(BF16)
    past = row < col

    def tile(kb, carry, masked):
        run, acc = carry
        start = pl.multiple_of(kb * tq, tq)
        k = k_ref[0, pl.ds(start, tq), :]
        z = _dot(k, qt)
        sp = _softplus(z)
        lk = -sp
        lb = z - sp
        if masked:
            lk = jnp.where(past, lk, 0.0)
        hi, lo = _split2(lk)
        rem = _dot(upper, hi) + _dot(upper, lo) + run
        a = jnp.exp(lb + rem)
        if masked:
            a = jnp.where(past, a, 0.0)
        acc = acc + _dot(v_ref[0, kb], a.astype(BF16))
        run = run + jnp.sum(lk, axis=0, keepdims=True)
        return run, acc

    init = (jnp.zeros((1, tq), F32), jnp.zeros((HEAD_DIM, tq), F32))
    carry = tile(qi, init, True)
    _, acc = lax.fori_loop(0, qi, lambda i, c: tile(qi - 1 - i, c, False), carry)
    _head_norm_store(acc, g_ref, o_ref)


def _attention(kernel, qt, k, vt, g_col, *, tq, name):
    b, d_slab, s = qt.shape
    nk = vt.shape[1]
    d_val = N_HEADS * HEAD_DIM
    return pl.pallas_call(
        functools.partial(kernel, tq=tq),
        out_shape=jax.ShapeDtypeStruct((b, d_val, s), BF16),
        grid=(b, N_HEADS, s // tq),
        in_specs=[pl.BlockSpec((1, SLAB, tq), lambda i, h, j: (i, h, j)),
                  pl.BlockSpec((1, s, SLAB), lambda i, h, j: (i, 0, h)),
                  pl.BlockSpec((1, nk, HEAD_DIM, tq), lambda i, h, j: (i, 0, h, 0)),
                  pl.BlockSpec((HEAD_DIM, 1), lambda i, h, j: (h, 0))],
        out_specs=pl.BlockSpec((1, HEAD_DIM, tq), lambda i, h, j: (i, h, j)),
        compiler_params=_cparams(("arbitrary", "arbitrary", "arbitrary")),
        name=name,
    )(qt, k, vt, g_col)


def _outproj_kernel(of_ref, os_ref, x_ref, gate_ref, sc_ref, sh_ref, wa_ref, wb_ref,
                    g_ref, b_ref, wr_ref, x1_ref, up_ref, lg_ref, *, alpha):
    d = x_ref.shape[2]
    mix = _dot_tn(of_ref[0], wa_ref[...]) + _dot_tn(os_ref[0], wb_ref[...])
    x1 = _layer_norm(alpha * x_ref[0] + (1.0 + gate_ref[0]) * mix, g_ref[...], b_ref[...])
    x1_ref[...] = x1
    u2 = x1 * (1.0 + sc_ref[0]) + sh_ref[0]
    lo = lax.bitcast_convert_type(u2[:, :d // 2].astype(BF16).astype(F32), U32)
    hi = lax.bitcast_convert_type(u2[:, d // 2:].astype(BF16).astype(F32), U32)
    up_ref[...] = (lo >> 16) | (hi & jnp.uint32(0xFFFF0000))
    lg_ref[...] = _dot3_nt(wr_ref[...], u2)


def _outproj(of_t, os_t, x, gate, sc, sh, wa, wb, g, bb, wr_t, *, t, alpha):
    b, s, d = x.shape
    n = b * s
    nt = s // t
    e = wr_t.shape[0]
    d_val = of_t.shape[1]
    mod_spec = pl.BlockSpec((1, 1, d), lambda i, j: (i, 0, 0))
    return pl.pallas_call(
        functools.partial(_outproj_kernel, alpha=alpha),
        out_shape=(jax.ShapeDtypeStruct((n, d), F32),
                   jax.ShapeDtypeStruct((n, d // 2), U32),
                   jax.ShapeDtypeStruct((e, n), F32)),
        grid=(b, nt),
        in_specs=[pl.BlockSpec((1, d_val, t), lambda i, j: (i, 0, j)),
                  pl.BlockSpec((1, d_val, t), lambda i, j: (i, 0, j)),
                  pl.BlockSpec((1, t, d), lambda i, j: (i, j, 0)),
                  mod_spec, mod_spec, mod_spec,
                  pl.BlockSpec(wa.shape, lambda i, j: (0, 0)),
                  pl.BlockSpec(wb.shape, lambda i, j: (0, 0)),
                  pl.BlockSpec((1, d), lambda i, j: (0, 0)),
                  pl.BlockSpec((1, d), lambda i, j: (0, 0)),
                  pl.BlockSpec(wr_t.shape, lambda i, j: (0, 0))],
        out_specs=(pl.BlockSpec((t, d), lambda i, j: (i * nt + j, 0)),
                   pl.BlockSpec((t, d // 2), lambda i, j: (i * nt + j, 0)),
                   pl.BlockSpec((e, t), lambda i, j: (0, i * nt + j))),
        compiler_params=_cparams(("arbitrary", "arbitrary")),
        name="outproj",
    )(of_t, os_t, x, gate, sc, sh, wa, wb, g, bb, wr_t)


def _route_kernel(lg_ref, bias_ref, idx_ref, rank_ref, w_ref, cnt_ref, *, t):
    @pl.when(pl.program_id(0) == 0)
    def _():
        cnt_ref[...] = jnp.zeros_like(cnt_ref)

    scores = jax.nn.sigmoid(lg_ref[...])
    sel = scores + bias_ref[...]
    neg_inf = -jnp.inf

    gscore = []
    for gi in range(N_GROUPS):
        blk = sel[gi * GROUP_SIZE:(gi + 1) * GROUP_SIZE, :]
        m1 = jnp.max(blk, axis=0, keepdims=True)
        is_max = blk == m1
        n_max = jnp.sum(jnp.where(is_max, 1.0, 0.0), axis=0, keepdims=True)
        m2 = jnp.max(jnp.where(is_max, neg_inf, blk), axis=0, keepdims=True)
        gscore.append(m1 + jnp.where(n_max >= 2.0, m1, m2))

    parts = []
    for gi in range(N_GROUPS):
        beaten = jnp.zeros_like(gscore[gi])
        for gj in range(N_GROUPS):
            if gj == gi:
                continue
            wins = (gscore[gj] > gscore[gi]) if gj > gi else (gscore[gj] >= gscore[gi])
            beaten = beaten + jnp.where(wins, 1.0, 0.0)
        keep = beaten < float(TOP_GROUPS)
        blk = sel[gi * GROUP_SIZE:(gi + 1) * GROUP_SIZE, :]
        parts.append(jnp.where(keep, blk, neg_inf))
    cand = jnp.concatenate(parts, axis=0)

    eidx = lax.broadcasted_iota(I32, cand.shape, 0).astype(F32)
    chosen = jnp.zeros(cand.shape, F32)
    idx_rows, w_rows = [], []
    for _ in range(TOP_K):
        mx = jnp.max(cand, axis=0, keepdims=True)
        idx = jnp.min(jnp.where(cand == mx, eidx, float(N_EXPERTS)), axis=0, keepdims=True)
        hit = eidx == idx
        w_rows.append(jnp.sum(jnp.where(hit, scores, 0.0), axis=0, keepdims=True))
        idx_rows.append(idx)
        cand = jnp.where(hit, neg_inf, cand)
        chosen = jnp.where(hit, 1.0, chosen)

    w_sum = w_rows[0]
    for wk in w_rows[1:]:
        w_sum = w_sum + wk

    r = lax.broadcasted_iota(I32, (t, t), 0)
    c = lax.broadcasted_iota(I32, (t, t), 1)
    before = jnp.where(r < c, 1.0, 0.0).astype(BF16)
    prefix = _dot(chosen.astype(BF16), before) + cnt_ref[...]
    for k in range(TOP_K):
        hit = eidx == idx_rows[k]
        rank = jnp.sum(jnp.where(hit, prefix, 0.0), axis=0, keepdims=True)
        idx_ref[k:k + 1, :] = idx_rows[k].astype(I32)
        rank_ref[k:k + 1, :] = rank.astype(I32)
        w_ref[k:k + 1, :] = w_rows[k] / w_sum * ROUTED_SCALE
    cnt_ref[...] = cnt_ref[...] + jnp.sum(chosen, axis=1, keepdims=True)


def _route(logits_t, bias_col, *, t):
    e, n = logits_t.shape
    row_spec = pl.BlockSpec((TOP_K, t), lambda i: (0, i))
    return pl.pallas_call(
        functools.partial(_route_kernel, t=t),
        out_shape=(jax.ShapeDtypeStruct((TOP_K, n), I32),
                   jax.ShapeDtypeStruct((TOP_K, n), I32),
                   jax.ShapeDtypeStruct((TOP_K, n), F32),
                   jax.ShapeDtypeStruct((e, 1), F32)),
        grid=(n // t,),
        in_specs=[pl.BlockSpec((e, t), lambda i: (0, i)),
                  pl.BlockSpec((e, 1), lambda i: (0, 0))],
        out_specs=(row_spec, row_spec, row_spec, pl.BlockSpec((e, 1), lambda i: (0, 0))),
        compiler_params=_cparams(("arbitrary",)),
        name="route",
    )(logits_t, bias_col)


def _dest_kernel(cnt_ref, idx_ref, rank_ref, dest_ref, be_ref, nb_ref, pstart_ref, *, n_blocks):
    e = cnt_ref.shape[0]

    @pl.when(pl.program_id(0) == 0)
    def _():
        nblk = jnp.floor((cnt_ref[...] + float(MOE_BLOCK - 1)) * (1.0 / MOE_BLOCK))
        hi = jnp.floor(nblk * (1.0 / 32.0))
        lo = nblk - 32.0 * hi
        r = lax.broadcasted_iota(I32, (e, e), 0)
        c = lax.broadcasted_iota(I32, (e, e), 1)
        below = jnp.where(c < r, 1.0, 0.0).astype(BF16)
        hi_b = jnp.broadcast_to(hi, (e, SLAB)).astype(BF16)
        lo_b = jnp.broadcast_to(lo, (e, SLAB)).astype(BF16)
        bstart = 32.0 * _dot(below, hi_b) + _dot(below, lo_b)
        pstart_ref[...] = bstart[:, 0:1] * float(MOE_BLOCK)
        bend = bstart[:, 0:1] + nblk
        bid = lax.broadcasted_iota(I32, (e, n_blocks), 1).astype(F32)
        owner = jnp.sum(jnp.where(bend <= bid, 1.0, 0.0), axis=0, keepdims=True)
        be_ref[...] = jnp.minimum(owner, float(e - 1)).astype(I32)
        nb_ref[...] = jnp.broadcast_to(jnp.max(bend, axis=0, keepdims=True), nb_ref.shape).astype(I32)

    eidx = lax.broadcasted_iota(I32, (e, idx_ref.shape[1]), 0)
    for k in range(TOP_K):
        hit = eidx == idx_ref[k:k + 1, :]
        base = jnp.sum(jnp.where(hit, pstart_ref[...], 0.0), axis=0, keepdims=True)
        dest_ref[k:k + 1, :] = base.astype(I32) + rank_ref[k:k + 1, :]


def _dest(counts, idx_t, rank_t, *, t, n_blocks):
    e = counts.shape[0]
    n = idx_t.shape[1]
    row_spec = pl.BlockSpec((TOP_K, t), lambda i: (0, i))
    return pl.pallas_call(
        functools.partial(_dest_kernel, n_blocks=n_blocks),
        out_shape=(jax.ShapeDtypeStruct((TOP_K, n), I32),
                   jax.ShapeDtypeStruct((1, n_blocks), I32),
                   jax.ShapeDtypeStruct((1, SLAB), I32)),
        grid=(n // t,),
        in_specs=[pl.BlockSpec((e, 1), lambda i: (0, 0)), row_spec, row_spec],
        out_specs=(row_spec,
                   pl.BlockSpec((1, n_blocks), lambda i: (0, 0)),
                   pl.BlockSpec((1, SLAB), lambda i: (0, 0))),
        scratch_shapes=[pltpu.VMEM((e, 1), F32)],
        compiler_params=_cparams(("arbitrary",)),
        name="dest",
    )(counts, idx_t, rank_t)


def _row_copy(src_ref, src_row, dst_ref, dst_row, sem):
    return pltpu.make_async_copy(src_ref.at[pl.ds(src_row, 1)], dst_ref.at[pl.ds(dst_row, 1)], sem)


def _dispatch_kernel(dest_ref, up_ref, xs_in_ref, xs_ref, sem, *, t):
    del xs_in_ref
    base = pl.program_id(0) * t

    def issue(j, _):
        for k in range(TOP_K):
            _row_copy(up_ref, base + j, xs_ref, dest_ref[k * t + j], sem).start()
        return 0

    lax.fori_loop(0, t, issue, 0)

    def drain(j, _):
        for k in range(TOP_K):
            _row_copy(up_ref, base + j, xs_ref, dest_ref[k * t + j], sem).wait()
        return 0

    lax.fori_loop(0, t, drain, 0)


def _dispatch(dest_flat, up, xs_zero, *, t):
    n = up.shape[0]
    return pl.pallas_call(
        functools.partial(_dispatch_kernel, t=t),
        out_shape=jax.ShapeDtypeStruct(xs_zero.shape, xs_zero.dtype),
        grid=(n // t,),
        in_specs=[pl.BlockSpec((TOP_K * t,), lambda i: (i,), memory_space=pltpu.SMEM),
                  pl.BlockSpec(memory_space=pl.ANY),
                  pl.BlockSpec(memory_space=pl.ANY)],
        out_specs=pl.BlockSpec(memory_space=pl.ANY),
        scratch_shapes=[pltpu.SemaphoreType.DMA(())],
        input_output_aliases={2: 0},
        compiler_params=_cparams(("arbitrary",)),
        name="dispatch",
    )(dest_flat, up, xs_zero)


def _unpack(words):
    lo = lax.bitcast_convert_type(words << 16, F32).astype(BF16)
    hi = lax.bitcast_convert_type(words & jnp.uint32(0xFFFF0000), F32).astype(BF16)
    return lo, hi


def _expert_kernel(be_ref, nb_ref, xs_ref, wg_ref, wu_ref, wd_ref, ys_ref, wgu_s, wd_s):
    blk = pl.program_id(0)
    d_half = xs_ref.shape[1]
    d_exp = wg_ref.shape[2]
    prev = be_ref[jnp.maximum(blk - 1, 0)]
    fresh = (blk == 0) | (prev != be_ref[blk])
    active = blk < nb_ref[0]

    @pl.when(active & fresh)
    def _():
        wgu_s[:, :d_exp] = wg_ref[0].astype(BF16)
        wgu_s[:, d_exp:] = wu_ref[0].astype(BF16)
        wd_s[...] = wd_ref[0].astype(BF16)

    @pl.when(active)
    def _():
        lo, hi = _unpack(xs_ref[...])
        gu = _dot(lo, wgu_s[:d_half, :]) + _dot(hi, wgu_s[d_half:, :])
        gate = gu[:, :d_exp]
        h = gate * jax.nn.sigmoid(gate) * gu[:, d_exp:]
        ys_ref[...] = _dot(h.astype(BF16), wd_s[...])

    @pl.when(jnp.logical_not(active))
    def _():
        ys_ref[...] = jnp.zeros_like(ys_ref)


def _experts(blk_expert, n_active, xs, wg, wu, wd):
    cap, d_half = xs.shape
    d = 2 * d_half
    d_exp = wg.shape[2]
    n_blocks = cap // MOE_BLOCK
    grid_spec = pltpu.PrefetchScalarGridSpec(
        num_scalar_prefetch=2,
        grid=(n_blocks,),
        in_specs=[pl.BlockSpec((MOE_BLOCK, d_half), lambda i, be, nb: (i, 0)),
                  pl.BlockSpec((1, d, d_exp), lambda i, be, nb: (be[i], 0, 0)),
                  pl.BlockSpec((1, d, d_exp), lambda i, be, nb: (be[i], 0, 0)),
                  pl.BlockSpec((1, d_exp, d), lambda i, be, nb: (be[i], 0, 0))],
        out_specs=pl.BlockSpec((MOE_BLOCK, d), lambda i, be, nb: (i, 0)),
        scratch_shapes=[pltpu.VMEM((d, 2 * d_exp), BF16), pltpu.VMEM((d_exp, d), BF16)],
    )
    return pl.pallas_call(
        _expert_kernel,
        out_shape=jax.ShapeDtypeStruct((cap, d), F32),
        grid_spec=grid_spec,
        compiler_params=_cparams(("arbitrary",)),
        name="experts",
    )(blk_expert, n_active, xs, wg, wu, wd)


def _combine_kernel(dest_ref, ys_ref, w_ref, x1_ref, gate_ref, sc_ref, sh_ref,
                    wgu_ref, wd_ref, g_ref, b_ref, o_ref, buf, sem, *, t, alpha):
    def issue(j, _):
        for k in range(TOP_K):
            _row_copy(ys_ref, dest_ref[k * t + j], buf.at[k], j, sem).start()
        return 0

    lax.fori_loop(0, t, issue, 0)

    x1 = x1_ref[...]
    ub = (x1 * (1.0 + sc_ref[0]) + sh_ref[0]).astype(BF16)
    d_sh = wd_ref.shape[0]
    gu = _dot(ub, wgu_ref[...])
    gate = gu[:, :d_sh]
    h = gate * jax.nn.sigmoid(gate) * gu[:, d_sh:]
    ffn = _dot(h.astype(BF16), wd_ref[...])

    def drain(j, _):
        for k in range(TOP_K):
            _row_copy(ys_ref, dest_ref[k * t + j], buf.at[k], j, sem).wait()
        return 0

    lax.fori_loop(0, t, drain, 0)

    w = w_ref[...]
    for k in range(TOP_K):
        ffn = ffn + w[:, k:k + 1] * buf[k]
    o_ref[...] = _layer_norm(alpha * x1 + (1.0 + gate_ref[0]) * ffn, g_ref[...], b_ref[...])


def _combine(dest_flat, ys, w_tok, x1, gate, sc, sh, wgu, wd, g, bb, *, t, seq, alpha):
    n, d = x1.shape
    per_batch = seq // t
    mod_spec = pl.BlockSpec((1, 1, d), lambda i: (i // per_batch, 0, 0))
    return pl.pallas_call(
        functools.partial(_combine_kernel, t=t, alpha=alpha),
        out_shape=jax.ShapeDtypeStruct((n, d), F32),
        grid=(n // t,),
        in_specs=[pl.BlockSpec((TOP_K * t,), lambda i: (i,), memory_space=pltpu.SMEM),
                  pl.BlockSpec(memory_space=pl.ANY),
                  pl.BlockSpec((t, TOP_K), lambda i: (i, 0)),
                  pl.BlockSpec((t, d), lambda i: (i, 0)),
                  mod_spec, mod_spec, mod_spec,
                  pl.BlockSpec(wgu.shape, lambda i: (0, 0)),
                  pl.BlockSpec(wd.shape, lambda i: (0, 0)),
                  pl.BlockSpec((1, d), lambda i: (0, 0)),
                  pl.BlockSpec((1, d), lambda i: (0, 0))],
        out_specs=pl.BlockSpec((t, d), lambda i: (i, 0)),
        scratch_shapes=[pltpu.VMEM((TOP_K, t, d), F32), pltpu.SemaphoreType.DMA(())],
        compiler_params=_cparams(("arbitrary",)),
        name="combine",
    )(dest_flat, ys, w_tok, x1, gate, sc, sh, wgu, wd, g, bb)


def _slab_cols(w, scale=1.0):
    d = w.shape[0]
    w = (w * scale).reshape(d, N_HEADS, HEAD_DIM)
    w = jnp.pad(w, ((0, 0), (0, 0), (0, SLAB - HEAD_DIM)))
    return w.reshape(d, N_HEADS * SLAB)


def _placement():
    h = jnp.arange(SLAB)[:, None]
    c = jnp.arange(N_HEADS * SLAB)[None, :]
    mats = [((h < N_HEADS) & (c == h * SLAB + HEAD_DIM + p)) for p in range(3)]
    return jnp.stack(mats).astype(BF16)


def _tile_major(rows, t):
    k, n = rows.shape
    return rows.reshape(k, n // t, t).transpose(1, 0, 2).reshape(-1)


def _layer(x, c_pad, w_ada, b_ada, w_in, b_f, fox_g, sb_g, w_out, ln1_g, ln1_b,
           w_router, router_bias, w_gate_e, w_up_e, w_down_e,
           w_gate_sh, w_up_sh, w_down_sh, ln2_g, ln2_b, *, alpha,
           t_proj=512, t_attn=256, t_route=512, t_moe=128):
    b, s, d = x.shape
    n = b * s
    d_val = N_HEADS * HEAD_DIM
    scale = HEAD_DIM ** -0.5

    ada = _ada(c_pad, w_ada, b_ada.reshape(1, -1))[:b]
    shift1, scale1, gate1, shift2, scale2, gate2 = [m[:, None, :] for m in jnp.split(ada, 6, axis=-1)]

    q_f, k_f, v_f, q_s, k_s, v_s, w_f = jnp.split(
        w_in, [d_val, 2 * d_val, 3 * d_val, 4 * d_val, 5 * d_val, 6 * d_val], axis=1)
    w_f = jnp.pad(w_f, ((0, 0), (0, SLAB - N_HEADS)))
    wtok = jnp.concatenate([_slab_cols(k_f), _slab_cols(k_s), w_f], axis=1).astype(BF16)
    wfeat = jnp.concatenate([_slab_cols(q_f, scale), _slab_cols(q_s, scale), v_f, v_s], axis=1).T.astype(BF16)
    bf_row = jnp.pad(b_f, (0, SLAB - N_HEADS)).reshape(1, SLAB)

    kf, ks, qf_t, qs_t, vf_t, vs_t = _inproj(x, scale1, shift1, wtok, wfeat, bf_row, _placement(),
                                             t=t_proj, tk=t_attn)
    of_t = _attention(_fox_kernel, qf_t, kf, vf_t, fox_g.reshape(-1, 1), tq=t_attn, name="fox")
    os_t = _attention(_sb_kernel, qs_t, ks, vs_t, sb_g.reshape(-1, 1), tq=t_attn, name="sb")

    x1, u_packed, logits_t = _outproj(
        of_t, os_t, x, gate1, scale2, shift2,
        w_out[:d_val].astype(BF16), w_out[d_val:].astype(BF16),
        ln1_g.reshape(1, d), ln1_b.reshape(1, d), w_router.T, t=t_proj, alpha=alpha)

    idx_t, rank_t, w_t, counts = _route(logits_t, router_bias.reshape(-1, 1), t=t_route)
    n_blocks = -(-(n * TOP_K) // MOE_BLOCK) + N_EXPERTS
    dest_t, blk_expert, n_active = _dest(counts, idx_t, rank_t, t=t_route, n_blocks=n_blocks)
    dest_flat = _tile_major(dest_t, t_moe)

    xs = _dispatch(dest_flat, u_packed, jnp.zeros((n_blocks * MOE_BLOCK, d // 2), U32), t=t_moe)
    ys = _experts(blk_expert.reshape(-1), n_active.reshape(-1)[:1], xs, w_gate_e, w_up_e, w_down_e)

    wgu_sh = jnp.concatenate([w_gate_sh, w_up_sh], axis=1).astype(BF16)
    out = _combine(dest_flat, ys, w_t.T, x1, gate2, scale2, shift2, wgu_sh, w_down_sh.astype(BF16),
                   ln2_g.reshape(1, d), ln2_b.reshape(1, d), t=t_moe, seq=s, alpha=alpha)
    return out.reshape(b, s, d)


def kernel(x, c, w_ada, b_ada, w_in, b_f, fox_norm_g, sb_norm_g, w_out, ln1_g, ln1_b, w_router, router_bias, w_gate_e, w_up_e, w_down_e, w_gate_sh, w_up_sh, w_down_sh, ln2_g, ln2_b):
    depth = w_ada.shape[0]
    alpha = (2.0 * depth) ** 0.25
    c_pad = jnp.pad(c, ((0, (-c.shape[0]) % 8), (0, 0)))
    for l in range(depth):
        x = _layer(x, c_pad, w_ada[l], b_ada[l], w_in[l], b_f[l], fox_norm_g[l], sb_norm_g[l],
                   w_out[l], ln1_g[l], ln1_b[l], w_router[l], router_bias[l],
                   w_gate_e[l], w_up_e[l], w_down_e[l], w_gate_sh[l], w_up_sh[l], w_down_sh[l],
                   ln2_g[l], ln2_b[l], alpha=alpha)
    return x
```

```python
import functools

import jax
import jax.numpy as jnp
from jax import lax
from jax.experimental import pallas as pl
from jax.experimental.pallas import tpu as pltpu

F32 = jnp.float32
BF16 = jnp.bfloat16
I32 = jnp.int32
U32 = jnp.uint32

HEAD_DIM = 64
N_HEADS = 8
SLAB = 128
N_EXPERTS = 256
TOP_K = 8
N_GROUPS = 8
TOP_GROUPS = 4
GROUP_SIZE = N_EXPERTS // N_GROUPS
ROUTED_SCALE = 2.5
MOE_BLOCK = 256
MOE_SPLIT = 2
SB_CHUNK = 128
LN_EPS = 1e-5
RMS_EPS = 1e-6
NEG_BIG = -1e30

VMEM_LIMIT = 56 * 1024 * 1024


def _cparams(sem):
    return pltpu.CompilerParams(dimension_semantics=sem, vmem_limit_bytes=VMEM_LIMIT)


def _dot(a, b):
    return jnp.dot(a, b, preferred_element_type=F32)


def _dot_nt(a, b):
    return lax.dot_general(a, b, (((1,), (1,)), ((), ())), preferred_element_type=F32)


def _dot_tn(a, b):
    return lax.dot_general(a, b, (((0,), (0,)), ((), ())), preferred_element_type=F32)


def _split2(x):
    hi = x.astype(BF16)
    lo = (x - hi.astype(F32)).astype(BF16)
    return hi, lo


def _split3(x):
    p1 = x.astype(BF16)
    r1 = x - p1.astype(F32)
    p2 = r1.astype(BF16)
    p3 = (r1 - p2.astype(F32)).astype(BF16)
    return p1, p2, p3


def _dot3(a, b):
    a_hi, a_lo = _split2(a)
    b_hi, b_lo = _split2(b)
    return _dot(a_hi, b_hi) + _dot(a_lo, b_hi) + _dot(a_hi, b_lo)


def _dot3_nt(a, b):
    a_hi, a_lo = _split2(a)
    b_hi, b_lo = _split2(b)
    return _dot_nt(a_hi, b_hi) + _dot_nt(a_lo, b_hi) + _dot_nt(a_hi, b_lo)


def _softplus(z):
    return jnp.maximum(z, 0.0) + jnp.log(1.0 + jnp.exp(-jnp.abs(z)))


def _layer_norm(v, g, b):
    mu = jnp.mean(v, axis=-1, keepdims=True)
    d = v - mu
    var = jnp.mean(d * d, axis=-1, keepdims=True)
    return d * lax.rsqrt(var + LN_EPS) * g + b


def _ada_kernel(c_ref, w_ref, b_ref, o_ref):
    c = c_ref[...]
    s = c * jax.nn.sigmoid(c)
    o_ref[...] = _dot3(s, w_ref[...]) + b_ref[...]


def _ada(c_pad, w, b):
    rows, d = c_pad.shape
    n = w.shape[1]
    tn = 1024
    return pl.pallas_call(
        _ada_kernel,
        out_shape=jax.ShapeDtypeStruct((rows, n), F32),
        grid=(n // tn,),
        in_specs=[pl.BlockSpec((rows, d), lambda j: (0, 0)),
                  pl.BlockSpec((d, tn), lambda j: (0, j)),
                  pl.BlockSpec((1, tn), lambda j: (0, j))],
        out_specs=pl.BlockSpec((rows, tn), lambda j: (0, j)),
        compiler_params=_cparams(("arbitrary",)),
        name="ada",
    )(c_pad, w, b)


def _inproj_kernel(x_ref, sc_ref, sh_ref, wtok_ref, wfeat_ref, bf_ref, place_ref,
                   kf_ref, ks_ref, qf_ref, qs_ref, vf_ref, vs_ref, carry_ref, *, t, tk):
    d_slab = N_HEADS * SLAB
    d_val = N_HEADS * HEAD_DIM

    @pl.when(pl.program_id(1) == 0)
    def _():
        carry_ref[...] = jnp.zeros_like(carry_ref)

    u = x_ref[0] * (1.0 + sc_ref[0]) + sh_ref[0]
    ub = u.astype(BF16)

    flog = _dot(ub, wtok_ref[:, 2 * d_slab:]) + bf_ref[...]
    logf = -_softplus(-flog)
    lane = lax.broadcasted_iota(I32, logf.shape, 1)
    logf = jnp.where(lane < N_HEADS, logf, 0.0)
    row = lax.broadcasted_iota(I32, (t, t), 0)
    col = lax.broadcasted_iota(I32, (t, t), 1)
    tri = jnp.where(row >= col, 1.0, 0.0).astype(BF16)
    lf_hi, lf_lo = _split2(logf)
    cum = _dot(tri, lf_hi) + _dot(tri, lf_lo) + carry_ref[...]
    carry_ref[...] = cum[t - 1:t, :]
    n1, n2, n3 = _split3(-cum)
    extras = _dot(n1, place_ref[0]) + _dot(n2, place_ref[1]) + _dot(n3, place_ref[2])

    kf_ref[0] = (_dot(ub, wtok_ref[:, :d_slab]) + extras).astype(BF16)
    ks_ref[0] = _dot(ub, wtok_ref[:, d_slab:2 * d_slab]).astype(BF16)

    qf = _dot_nt(wfeat_ref[:d_slab, :], ub)
    r = lax.broadcasted_iota(I32, qf.shape, 0) % SLAB
    qf = jnp.where((r >= HEAD_DIM) & (r < HEAD_DIM + 3), 1.0, qf)
    qf_ref[0] = qf.astype(BF16)
    qs_ref[0] = _dot_nt(wfeat_ref[d_slab:2 * d_slab, :], ub).astype(BF16)
    vf = _dot_nt(wfeat_ref[2 * d_slab:2 * d_slab + d_val, :], ub).astype(BF16)
    vs = _dot_nt(wfeat_ref[2 * d_slab + d_val:, :], ub).astype(BF16)
    for c in range(t // tk):
        vf_ref[0, c] = vf[:, c * tk:(c + 1) * tk]
        vs_ref[0, c] = vs[:, c * tk:(c + 1) * tk]


def _inproj(x, sc, sh, wtok, wfeat, bf_row, place, *, t, tk):
    b, s, d = x.shape
    d_slab = N_HEADS * SLAB
    d_val = N_HEADS * HEAD_DIM
    nt = s // t
    out_shape = (
        jax.ShapeDtypeStruct((b, s, d_slab), BF16),
        jax.ShapeDtypeStruct((b, s, d_slab), BF16),
        jax.ShapeDtypeStruct((b, d_slab, s), BF16),
        jax.ShapeDtypeStruct((b, d_slab, s), BF16),
        jax.ShapeDtypeStruct((b, s // tk, d_val, tk), BF16),
        jax.ShapeDtypeStruct((b, s // tk, d_val, tk), BF16),
    )
    tok_spec = pl.BlockSpec((1, t, d_slab), lambda i, j: (i, j, 0))
    feat_spec = pl.BlockSpec((1, d_slab, t), lambda i, j: (i, 0, j))
    val_spec = pl.BlockSpec((1, t // tk, d_val, tk), lambda i, j: (i, j, 0, 0))
    return pl.pallas_call(
        functools.partial(_inproj_kernel, t=t, tk=tk),
        out_shape=out_shape,
        grid=(b, nt),
        in_specs=[pl.BlockSpec((1, t, d), lambda i, j: (i, j, 0)),
                  pl.BlockSpec((1, 1, d), lambda i, j: (i, 0, 0)),
                  pl.BlockSpec((1, 1, d), lambda i, j: (i, 0, 0)),
                  pl.BlockSpec(wtok.shape, lambda i, j: (0, 0)),
                  pl.BlockSpec(wfeat.shape, lambda i, j: (0, 0)),
                  pl.BlockSpec((1, SLAB), lambda i, j: (0, 0)),
                  pl.BlockSpec(place.shape, lambda i, j: (0, 0, 0))],
        out_specs=(tok_spec, tok_spec, feat_spec, feat_spec, val_spec, val_spec),
        scratch_shapes=[pltpu.VMEM((1, SLAB), F32)],
        compiler_params=_cparams(("arbitrary", "arbitrary")),
        name="inproj",
    )(x, sc, sh, wtok, wfeat, bf_row, place)


def _head_norm_store(acc, g_ref, o_ref, h):
    rows = slice(h * HEAD_DIM, (h + 1) * HEAD_DIM)
    ms = jnp.mean(acc * acc, axis=0, keepdims=True)
    o_ref[0, rows, :] = (acc * lax.rsqrt(ms + RMS_EPS) * g_ref[rows, :]).astype(BF16)


def _fox_kernel(q_ref, k_ref, v_ref, g_ref, o_ref, *, tq, heads):
    qi = pl.program_id(2)

    def tile(kb, carry, masked):
        start = pl.multiple_of(kb * tq, tq)
        k_all = k_ref[0, pl.ds(start, tq), :]
        v_all = v_ref[0, kb]
        out = []
        for h in range(heads):
            m, l, acc = carry[h]
            qt = q_ref[0, h * SLAB:(h + 1) * SLAB, :]
            s = _dot(k_all[:, h * SLAB:(h + 1) * SLAB], qt)
            if masked:
                kid = lax.broadcasted_iota(I32, s.shape, 0)
                qid = lax.broadcasted_iota(I32, s.shape, 1)
                s = jnp.where(kid <= qid, s, NEG_BIG)
            m_new = jnp.maximum(m, jnp.max(s, axis=0, keepdims=True))
            p = jnp.exp(s - m_new)
            alpha = jnp.exp(m - m_new)
            l = alpha * l + jnp.sum(p, axis=0, keepdims=True)
            v = v_all[h * HEAD_DIM:(h + 1) * HEAD_DIM, :]
            acc = alpha * acc + _dot(v, p.astype(BF16))
            out.append((m_new, l, acc))
        return tuple(out)

    init = tuple((jnp.full((1, tq), NEG_BIG, F32), jnp.zeros((1, tq), F32),
                  jnp.zeros((HEAD_DIM, tq), F32)) for _ in range(heads))
    carry = lax.fori_loop(0, qi, lambda kb, c: tile(kb, c, False), init)
    final = tile(qi, carry, True)
    for h in range(heads):
        _, l, acc = final[h]
        _head_norm_store(acc / l, g_ref, o_ref, h)


def _sb_kernel(q_ref, k_ref, v_ref, g_ref, o_ref, *, tq, heads):
    qi = pl.program_id(2)
    cb = SB_CHUNK
    r = lax.broadcasted_iota(I32, (cb + 16, 2 * cb), 0)
    c = lax.broadcasted_iota(I32, (cb + 16, 2 * cb), 1) % cb
    uu = jnp.where(((r < cb) & (c > r)) | (r == cb), 1.0, 0.0).astype(BF16)
    row = lax.broadcasted_iota(I32, (tq, tq), 0)
    col = lax.broadcasted_iota(I32, (tq, tq), 1)
    past = row < col

    def tile(kb, carry, masked):
        start = pl.multiple_of(kb * tq, tq)
        k_all = k_ref[0, pl.ds(start, tq), :]
        v_all = v_ref[0, kb]
        out = []
        for h in range(heads):
            run, acc = carry[h]
            qt = q_ref[0, h * SLAB:(h + 1) * SLAB, :]
            z = _dot(k_all[:, h * SLAB:(h + 1) * SLAB], qt)
            sp = _softplus(z)
            lb = z - sp
            if masked:
                sp = jnp.where(past, sp, 0.0)
            chunks = [None] * (tq // cb)
            for ci in reversed(range(tq // cb)):
                rows = slice(ci * cb, (ci + 1) * cb)
                hi, lo = _split2(sp[rows, :])
                ext = _dot(uu, jnp.concatenate([hi, lo], axis=0))
                a = jnp.exp(lb[rows, :] - ext[:cb, :] - run)
                if masked:
                    a = jnp.where(past[rows, :], a, 0.0)
                chunks[ci] = a.astype(BF16)
                run = run + ext[cb:cb + 1, :]
            v = v_all[h * HEAD_DIM:(h + 1) * HEAD_DIM, :]
            acc = acc + _dot(v, jnp.concatenate(chunks, axis=0))
            out.append((run, acc))
        return tuple(out)

    init = tuple((jnp.zeros((1, tq), F32), jnp.zeros((HEAD_DIM, tq), F32)) for _ in range(heads))
    carry = tile(qi, init, True)
    final = lax.fori_loop(0, qi, lambda i, cr: tile(qi - 1 - i, cr, False), carry)
    for h in range(heads):
        _head_norm_store(final[h][1], g_ref, o_ref, h)


def _attention(kernel, qt, k, vt, g_col, *, tq, heads, name):
    b, d_slab, s = qt.shape
    nk = vt.shape[1]
    d_val = N_HEADS * HEAD_DIM
    return pl.pallas_call(
        functools.partial(kernel, tq=tq, heads=heads),
        out_shape=jax.ShapeDtypeStruct((b, d_val, s), BF16),
        grid=(b, N_HEADS // heads, s // tq),
        in_specs=[pl.BlockSpec((1, heads * SLAB, tq), lambda i, h, j: (i, h, j)),
                  pl.BlockSpec((1, s, heads * SLAB), lambda i, h, j: (i, 0, h)),
                  pl.BlockSpec((1, nk, heads * HEAD_DIM, tq), lambda i, h, j: (i, 0, h, 0)),
                  pl.BlockSpec((heads * HEAD_DIM, 1), lambda i, h, j: (h, 0))],
        out_specs=pl.BlockSpec((1, heads * HEAD_DIM, tq), lambda i, h, j: (i, h, j)),
        compiler_params=_cparams(("arbitrary", "arbitrary", "arbitrary")),
        name=name,
    )(qt, k, vt, g_col)


def _outproj_kernel(of_ref, os_ref, x_ref, gate_ref, sc_ref, sh_ref, wa_ref, wb_ref,
                    g_ref, b_ref, wr_ref, x1_ref, up_ref, lg_ref, *, alpha):
    d = x_ref.shape[2]
    mix = _dot_tn(of_ref[0], wa_ref[...]) + _dot_tn(os_ref[0], wb_ref[...])
    x1 = _layer_norm(alpha * x_ref[0] + (1.0 + gate_ref[0]) * mix, g_ref[...], b_ref[...])
    x1_ref[...] = x1
    u2 = x1 * (1.0 + sc_ref[0]) + sh_ref[0]
    lo = lax.bitcast_convert_type(u2[:, :d // 2].astype(BF16).astype(F32), U32)
    hi = lax.bitcast_convert_type(u2[:, d // 2:].astype(BF16).astype(F32), U32)
    up_ref[...] = (lo >> 16) | (hi & jnp.uint32(0xFFFF0000))
    lg_ref[...] = _dot3_nt(wr_ref[...], u2)


def _outproj(of_t, os_t, x, gate, sc, sh, wa, wb, g, bb, wr_t, *, t, alpha):
    b, s, d = x.shape
    n = b * s
    nt = s // t
    e = wr_t.shape[0]
    d_val = of_t.shape[1]
    mod_spec = pl.BlockSpec((1, 1, d), lambda i, j: (i, 0, 0))
    return pl.pallas_call(
        functools.partial(_outproj_kernel, alpha=alpha),
        out_shape=(jax.ShapeDtypeStruct((n, d), F32),
                   jax.ShapeDtypeStruct((n, d // 2), U32),
                   jax.ShapeDtypeStruct((e, n), F32)),
        grid=(b, nt),
        in_specs=[pl.BlockSpec((1, d_val, t), lambda i, j: (i, 0, j)),
                  pl.BlockSpec((1, d_val, t), lambda i, j: (i, 0, j)),
                  pl.BlockSpec((1, t, d), lambda i, j: (i, j, 0)),
                  mod_spec, mod_spec, mod_spec,
                  pl.BlockSpec(wa.shape, lambda i, j: (0, 0)),
                  pl.BlockSpec(wb.shape, lambda i, j: (0, 0)),
                  pl.BlockSpec((1, d), lambda i, j: (0, 0)),
                  pl.BlockSpec((1, d), lambda i, j: (0, 0)),
                  pl.BlockSpec(wr_t.shape, lambda i, j: (0, 0))],
        out_specs=(pl.BlockSpec((t, d), lambda i, j: (i * nt + j, 0)),
                   pl.BlockSpec((t, d // 2), lambda i, j: (i * nt + j, 0)),
                   pl.BlockSpec((e, t), lambda i, j: (0, i * nt + j))),
        compiler_params=_cparams(("arbitrary", "arbitrary")),
        name="outproj",
    )(of_t, os_t, x, gate, sc, sh, wa, wb, g, bb, wr_t)


def _route_kernel(lg_ref, bias_ref, idx_ref, rank_ref, w_ref, cnt_ref, *, t):
    @pl.when(pl.program_id(0) == 0)
    def _():
        cnt_ref[...] = jnp.zeros_like(cnt_ref)

    scores = jax.nn.sigmoid(lg_ref[...])
    sel = scores + bias_ref[...]
    neg_inf = -jnp.inf

    gscore = []
    for gi in range(N_GROUPS):
        blk = sel[gi * GROUP_SIZE:(gi + 1) * GROUP_SIZE, :]
        m1 = jnp.max(blk, axis=0, keepdims=True)
        is_max = blk == m1
        n_max = jnp.sum(jnp.where(is_max, 1.0, 0.0), axis=0, keepdims=True)
        m2 = jnp.max(jnp.where(is_max, neg_inf, blk), axis=0, keepdims=True)
        gscore.append(m1 + jnp.where(n_max >= 2.0, m1, m2))

    parts = []
    for gi in range(N_GROUPS):
        beaten = jnp.zeros_like(gscore[gi])
        for gj in range(N_GROUPS):
            if gj == gi:
                continue
            wins = (gscore[gj] > gscore[gi]) if gj > gi else (gscore[gj] >= gscore[gi])
            beaten = beaten + jnp.where(wins, 1.0, 0.0)
        keep = beaten < float(TOP_GROUPS)
        blk = sel[gi * GROUP_SIZE:(gi + 1) * GROUP_SIZE, :]
        parts.append(jnp.where(keep, blk, neg_inf))
    cand = jnp.concatenate(parts, axis=0)

    eidx = lax.broadcasted_iota(I32, cand.shape, 0).astype(F32)
    chosen = jnp.zeros(cand.shape, F32)
    idx_rows, w_rows = [], []
    for _ in range(TOP_K):
        mx = jnp.max(cand, axis=0, keepdims=True)
        idx = jnp.min(jnp.where(cand == mx, eidx, float(N_EXPERTS)), axis=0, keepdims=True)
        hit = eidx == idx
        w_rows.append(jnp.sum(jnp.where(hit, scores, 0.0), axis=0, keepdims=True))
        idx_rows.append(idx)
        cand = jnp.where(hit, neg_inf, cand)
        chosen = jnp.where(hit, 1.0, chosen)

    w_sum = w_rows[0]
    for wk in w_rows[1:]:
        w_sum = w_sum + wk

    r = lax.broadcasted_iota(I32, (t, t), 0)
    c = lax.broadcasted_iota(I32, (t, t), 1)
    before = jnp.where(r < c, 1.0, 0.0).astype(BF16)
    prefix = _dot(chosen.astype(BF16), before) + cnt_ref[...]
    for k in range(TOP_K):
        hit = eidx == idx_rows[k]
        rank = jnp.sum(jnp.where(hit, prefix, 0.0), axis=0, keepdims=True)
        idx_ref[k:k + 1, :] = idx_rows[k].astype(I32)
        rank_ref[k:k + 1, :] = rank.astype(I32)
        w_ref[k:k + 1, :] = w_rows[k] / w_sum * ROUTED_SCALE
    cnt_ref[...] = cnt_ref[...] + jnp.sum(chosen, axis=1, keepdims=True)


def _route(logits_t, bias_col, *, t):
    e, n = logits_t.shape
    row_spec = pl.BlockSpec((TOP_K, t), lambda i: (0, i))
    return pl.pallas_call(
        functools.partial(_route_kernel, t=t),
        out_shape=(jax.ShapeDtypeStruct((TOP_K, n), I32),
                   jax.ShapeDtypeStruct((TOP_K, n), I32),
                   jax.ShapeDtypeStruct((TOP_K, n), F32),
                   jax.ShapeDtypeStruct((e, 1), F32)),
        grid=(n // t,),
        in_specs=[pl.BlockSpec((e, t), lambda i: (0, i)),
                  pl.BlockSpec((e, 1), lambda i: (0, 0))],
        out_specs=(row_spec, row_spec, row_spec, pl.BlockSpec((e, 1), lambda i: (0, 0))),
        compiler_params=_cparams(("arbitrary",)),
        name="route",
    )(logits_t, bias_col)


def _dest_kernel(cnt_ref, idx_ref, rank_ref, dest_ref, be_ref, nb_ref, pstart_ref, *, n_blocks):
    e = cnt_ref.shape[0]

    @pl.when(pl.program_id(0) == 0)
    def _():
        nblk = jnp.floor((cnt_ref[...] + float(MOE_BLOCK - 1)) * (1.0 / MOE_BLOCK))
        hi = jnp.floor(nblk * (1.0 / 32.0))
        lo = nblk - 32.0 * hi
        r = lax.broadcasted_iota(I32, (e, e), 0)
        c = lax.broadcasted_iota(I32, (e, e), 1)
        below = jnp.where(c < r, 1.0, 0.0).astype(BF16)
        hi_b = jnp.broadcast_to(hi, (e, SLAB)).astype(BF16)
        lo_b = jnp.broadcast_to(lo, (e, SLAB)).astype(BF16)
        bstart = 32.0 * _dot(below, hi_b) + _dot(below, lo_b)
        pstart_ref[...] = bstart[:, 0:1] * float(MOE_BLOCK)
        bend = bstart[:, 0:1] + nblk
        bid = lax.broadcasted_iota(I32, (e, n_blocks), 1).astype(F32)
        owner = jnp.sum(jnp.where(bend <= bid, 1.0, 0.0), axis=0, keepdims=True)
        be_ref[...] = jnp.minimum(owner, float(e - 1)).astype(I32)
        nb_ref[...] = jnp.broadcast_to(jnp.max(bend, axis=0, keepdims=True), nb_ref.shape).astype(I32)

    eidx = lax.broadcasted_iota(I32, (e, idx_ref.shape[1]), 0)
    for k in range(TOP_K):
        hit = eidx == idx_ref[k:k + 1, :]
        base = jnp.sum(jnp.where(hit, pstart_ref[...], 0.0), axis=0, keepdims=True)
        dest_ref[k:k + 1, :] = base.astype(I32) + rank_ref[k:k + 1, :]


def _dest(counts, idx_t, rank_t, *, t, n_blocks):
    e = counts.shape[0]
    n = idx_t.shape[1]
    row_spec = pl.BlockSpec((TOP_K, t), lambda i: (0, i))
    return pl.pallas_call(
        functools.partial(_dest_kernel, n_blocks=n_blocks),
        out_shape=(jax.ShapeDtypeStruct((TOP_K, n), I32),
                   jax.ShapeDtypeStruct((1, n_blocks), I32),
                   jax.ShapeDtypeStruct((1, SLAB), I32)),
        grid=(n // t,),
        in_specs=[pl.BlockSpec((e, 1), lambda i: (0, 0)), row_spec, row_spec],
        out_specs=(row_spec,
                   pl.BlockSpec((1, n_blocks), lambda i: (0, 0)),
                   pl.BlockSpec((1, SLAB), lambda i: (0, 0))),
        scratch_shapes=[pltpu.VMEM((e, 1), F32)],
        compiler_params=_cparams(("arbitrary",)),
        name="dest",
    )(counts, idx_t, rank_t)


def _row_copy(src_ref, src_row, dst_ref, dst_row, sem):
    return pltpu.make_async_copy(src_ref.at[pl.ds(src_row, 1)], dst_ref.at[pl.ds(dst_row, 1)], sem)


def _dispatch_kernel(dest_ref, up_ref, xs_in_ref, xs_ref, sem, *, t):
    del xs_in_ref

    def issue(j, _):
        for k in range(TOP_K):
            _row_copy(up_ref, j, xs_ref, dest_ref[k * t + j], sem).start()
        return 0

    lax.fori_loop(0, t, issue, 0)

    def drain(j, _):
        for k in range(TOP_K):
            _row_copy(up_ref, j, xs_ref, dest_ref[k * t + j], sem).wait()
        return 0

    lax.fori_loop(0, t, drain, 0)


def _dispatch(dest_flat, up, xs_zero, *, t):
    n = up.shape[0]
    return pl.pallas_call(
        functools.partial(_dispatch_kernel, t=t),
        out_shape=jax.ShapeDtypeStruct(xs_zero.shape, xs_zero.dtype),
        grid=(n // t,),
        in_specs=[pl.BlockSpec((TOP_K * t,), lambda i: (i,), memory_space=pltpu.SMEM),
                  pl.BlockSpec((t, up.shape[1]), lambda i: (i, 0)),
                  pl.BlockSpec(memory_space=pl.ANY)],
        out_specs=pl.BlockSpec(memory_space=pl.ANY),
        scratch_shapes=[pltpu.SemaphoreType.DMA(())],
        input_output_aliases={2: 0},
        compiler_params=_cparams(("arbitrary",)),
        name="dispatch",
    )(dest_flat, up, xs_zero)


def _unpack(words):
    lo = lax.bitcast_convert_type(words << 16, F32).astype(BF16)
    hi = lax.bitcast_convert_type(words & jnp.uint32(0xFFFF0000), F32).astype(BF16)
    return lo, hi


def _expert_kernel(be_ref, nb_ref, xs_ref, wg_ref, wu_ref, wd_ref, ys_ref, wgu_s, wd_s):
    blk = pl.program_id(0)
    d_half = xs_ref.shape[1]
    d_exp = wg_ref.shape[2]
    prev = be_ref[jnp.maximum(blk - 1, 0)]
    fresh = (blk == 0) | (prev != be_ref[blk])
    active = blk < nb_ref[0]

    @pl.when(active & fresh)
    def _():
        wgu_s[:, :d_exp] = wg_ref[0].astype(BF16)
        wgu_s[:, d_exp:] = wu_ref[0].astype(BF16)
        wd_s[...] = wd_ref[0].astype(BF16)

    @pl.when(active)
    def _():
        rows = MOE_BLOCK // MOE_SPLIT
        for part in range(MOE_SPLIT):
            sl = slice(part * rows, (part + 1) * rows)
            lo, hi = _unpack(xs_ref[sl, :])
            gu = _dot(lo, wgu_s[:d_half, :]) + _dot(hi, wgu_s[d_half:, :])
            gate = gu[:, :d_exp]
            h = gate * jax.nn.sigmoid(gate) * gu[:, d_exp:]
            ys_ref[sl, :] = _dot(h.astype(BF16), wd_s[...])

    @pl.when(jnp.logical_not(active))
    def _():
        ys_ref[...] = jnp.zeros_like(ys_ref)


def _experts(blk_expert, n_active, xs, wg, wu, wd):
    cap, d_half = xs.shape
    d = 2 * d_half
    d_exp = wg.shape[2]
    n_blocks = cap // MOE_BLOCK
    grid_spec = pltpu.PrefetchScalarGridSpec(
        num_scalar_prefetch=2,
        grid=(n_blocks,),
        in_specs=[pl.BlockSpec((MOE_BLOCK, d_half), lambda i, be, nb: (i, 0)),
                  pl.BlockSpec((1, d, d_exp), lambda i, be, nb: (be[i], 0, 0)),
                  pl.BlockSpec((1, d, d_exp), lambda i, be, nb: (be[i], 0, 0)),
                  pl.BlockSpec((1, d_exp, d), lambda i, be, nb: (be[i], 0, 0))],
        out_specs=pl.BlockSpec((MOE_BLOCK, d), lambda i, be, nb: (i, 0)),
        scratch_shapes=[pltpu.VMEM((d, 2 * d_exp), BF16), pltpu.VMEM((d_exp, d), BF16)],
    )
    return pl.pallas_call(
        _expert_kernel,
        out_shape=jax.ShapeDtypeStruct((cap, d), F32),
        grid_spec=grid_spec,
        compiler_params=_cparams(("arbitrary",)),
        name="experts",
    )(blk_expert, n_active, xs, wg, wu, wd)


def _combine_kernel(dest_ref, ys_ref, w_ref, x1_ref, gate_ref, sc_ref, sh_ref,
                    wgu_ref, wd_ref, g_ref, b_ref, o_ref, buf, sem, *, t, alpha):
    def issue(j, _):
        for k in range(TOP_K):
            _row_copy(ys_ref, dest_ref[k * t + j], buf.at[k], j, sem).start()
        return 0

    lax.fori_loop(0, t, issue, 0)

    x1 = x1_ref[...]
    ub = (x1 * (1.0 + sc_ref[0]) + sh_ref[0]).astype(BF16)
    d_sh = wd_ref.shape[0]
    gu = _dot(ub, wgu_ref[...])
    gate = gu[:, :d_sh]
    h = gate * jax.nn.sigmoid(gate) * gu[:, d_sh:]
    ffn = _dot(h.astype(BF16), wd_ref[...])

    def drain(j, _):
        for k in range(TOP_K):
            _row_copy(ys_ref, dest_ref[k * t + j], buf.at[k], j, sem).wait()
        return 0

    lax.fori_loop(0, t, drain, 0)

    w = w_ref[...]
    for k in range(TOP_K):
        ffn = ffn + w[:, k:k + 1] * buf[k]
    o_ref[...] = _layer_norm(alpha * x1 + (1.0 + gate_ref[0]) * ffn, g_ref[...], b_ref[...])


def _combine(dest_flat, ys, w_tok, x1, gate, sc, sh, wgu, wd, g, bb, *, t, seq, alpha):
    n, d = x1.shape
    per_batch = seq // t
    mod_spec = pl.BlockSpec((1, 1, d), lambda i: (i // per_batch, 0, 0))
    return pl.pallas_call(
        functools.partial(_combine_kernel, t=t, alpha=alpha),
        out_shape=jax.ShapeDtypeStruct((n, d), F32),
        grid=(n // t,),
        in_specs=[pl.BlockSpec((TOP_K * t,), lambda i: (i,), memory_space=pltpu.SMEM),
                  pl.BlockSpec(memory_space=pl.ANY),
                  pl.BlockSpec((t, TOP_K), lambda i: (i, 0)),
                  pl.BlockSpec((t, d), lambda i: (i, 0)),
                  mod_spec, mod_spec, mod_spec,
                  pl.BlockSpec(wgu.shape, lambda i: (0, 0)),
                  pl.BlockSpec(wd.shape, lambda i: (0, 0)),
                  pl.BlockSpec((1, d), lambda i: (0, 0)),
                  pl.BlockSpec((1, d), lambda i: (0, 0))],
        out_specs=pl.BlockSpec((t, d), lambda i: (i, 0)),
        scratch_shapes=[pltpu.VMEM((TOP_K, t, d), F32), pltpu.SemaphoreType.DMA(())],
        compiler_params=_cparams(("arbitrary",)),
        name="combine",
    )(dest_flat, ys, w_tok, x1, gate, sc, sh, wgu, wd, g, bb)


def _slab_cols(w, scale=1.0):
    d = w.shape[0]
    w = (w * scale).reshape(d, N_HEADS, HEAD_DIM)
    w = jnp.pad(w, ((0, 0), (0, 0), (0, SLAB - HEAD_DIM)))
    return w.reshape(d, N_HEADS * SLAB)


def _placement():
    h = jnp.arange(SLAB)[:, None]
    c = jnp.arange(N_HEADS * SLAB)[None, :]
    mats = [((h < N_HEADS) & (c == h * SLAB + HEAD_DIM + p)) for p in range(3)]
    return jnp.stack(mats).astype(BF16)


def _tile_major(rows, t):
    k, n = rows.shape
    return rows.reshape(k, n // t, t).transpose(1, 0, 2).reshape(-1)


def _layer(x, c_pad, w_ada, b_ada, w_in, b_f, fox_g, sb_g, w_out, ln1_g, ln1_b,
           w_router, router_bias, w_gate_e, w_up_e, w_down_e,
           w_gate_sh, w_up_sh, w_down_sh, ln2_g, ln2_b, *, alpha,
           t_proj=512, t_attn=512, attn_heads=2, t_route=512, t_moe=128):
    b, s, d = x.shape
    n = b * s
    d_val = N_HEADS * HEAD_DIM
    scale = HEAD_DIM ** -0.5

    ada = _ada(c_pad, w_ada, b_ada.reshape(1, -1))[:b]
    shift1, scale1, gate1, shift2, scale2, gate2 = [m[:, None, :] for m in jnp.split(ada, 6, axis=-1)]

    q_f, k_f, v_f, q_s, k_s, v_s, w_f = jnp.split(
        w_in, [d_val, 2 * d_val, 3 * d_val, 4 * d_val, 5 * d_val, 6 * d_val], axis=1)
    w_f = jnp.pad(w_f, ((0, 0), (0, SLAB - N_HEADS)))
    wtok = jnp.concatenate([_slab_cols(k_f), _slab_cols(k_s), w_f], axis=1).astype(BF16)
    wfeat = jnp.concatenate([_slab_cols(q_f, scale), _slab_cols(q_s, scale), v_f, v_s], axis=1).T.astype(BF16)
    bf_row = jnp.pad(b_f, (0, SLAB - N_HEADS)).reshape(1, SLAB)

    kf, ks, qf_t, qs_t, vf_t, vs_t = _inproj(x, scale1, shift1, wtok, wfeat, bf_row, _placement(),
                                             t=t_proj, tk=t_attn)
    of_t = _attention(_fox_kernel, qf_t, kf, vf_t, fox_g.reshape(-1, 1),
                      tq=t_attn, heads=attn_heads, name="fox")
    os_t = _attention(_sb_kernel, qs_t, ks, vs_t, sb_g.reshape(-1, 1),
                      tq=t_attn, heads=attn_heads, name="sb")

    x1, u_packed, logits_t = _outproj(
        of_t, os_t, x, gate1, scale2, shift2,
        w_out[:d_val].astype(BF16), w_out[d_val:].astype(BF16),
        ln1_g.reshape(1, d), ln1_b.reshape(1, d), w_router.T, t=t_proj, alpha=alpha)

    idx_t, rank_t, w_t, counts = _route(logits_t, router_bias.reshape(-1, 1), t=t_route)
    n_blocks = -(-(n * TOP_K) // MOE_BLOCK) + N_EXPERTS
    dest_t, blk_expert, n_active = _dest(counts, idx_t, rank_t, t=t_route, n_blocks=n_blocks)
    dest_flat = _tile_major(dest_t, t_moe)

    xs = _dispatch(dest_flat, u_packed, jnp.zeros((n_blocks * MOE_BLOCK, d // 2), U32), t=t_moe)
    ys = _experts(blk_expert.reshape(-1), n_active.reshape(-1)[:1], xs, w_gate_e, w_up_e, w_down_e)

    wgu_sh = jnp.concatenate([w_gate_sh, w_up_sh], axis=1).astype(BF16)
    out = _combine(dest_flat, ys, w_t.T, x1, gate2, scale2, shift2, wgu_sh, w_down_sh.astype(BF16),
                   ln2_g.reshape(1, d), ln2_b.reshape(1, d), t=t_moe, seq=s, alpha=alpha)
    return out.reshape(b, s, d)


def kernel(x, c, w_ada, b_ada, w_in, b_f, fox_norm_g, sb_norm_g, w_out, ln1_g, ln1_b, w_router, router_bias, w_gate_e, w_up_e, w_down_e, w_gate_sh, w_up_sh, w_down_sh, ln2_g, ln2_b):
    depth = w_ada.shape[0]
    alpha = (2.0 * depth) ** 0.25
    c_pad = jnp.pad(c, ((0, (-c.shape[0]) % 8), (0, 0)))
    for l in range(depth):
        x = _layer(x, c_pad, w_ada[l], b_ada[l], w_in[l], b_f[l], fox_norm_g[l], sb_norm_g[l],
                   w_out[l], ln1_g[l], ln1_b[l], w_router[l], router_bias[l],
                   w_gate_e[l], w_up_e[l], w_down_e[l], w_gate_sh[l], w_up_sh[l], w_down_sh[l],
                   ln2_g[l], ln2_b[l], alpha=alpha)
    return x
```

```python
import functools

import jax
import jax.numpy as jnp
from jax import lax
from jax.experimental import pallas as pl
from jax.experimental.pallas import tpu as pltpu

F32 = jnp.float32
BF16 = jnp.bfloat16
I32 = jnp.int32
U32 = jnp.uint32

HEAD_DIM = 64
N_HEADS = 8
SLAB = 128
N_EXPERTS = 256
TOP_K = 8
N_GROUPS = 8
TOP_GROUPS = 4
GROUP_SIZE = N_EXPERTS // N_GROUPS
ROUTED_SCALE = 2.5
MOE_BLOCK = 256
MOE_SPLIT = 2
SB_CHUNK = 128
LN_EPS = 1e-5
RMS_EPS = 1e-6
NEG_BIG = -1e30
LOG2_E = 1.4426950408889634

VMEM_LIMIT = 56 * 1024 * 1024


def _cparams(sem):
    return pltpu.CompilerParams(dimension_semantics=sem, vmem_limit_bytes=VMEM_LIMIT)


def _dot(a, b):
    return jnp.dot(a, b, preferred_element_type=F32)


def _dot_nt(a, b):
    return lax.dot_general(a, b, (((1,), (1,)), ((), ())), preferred_element_type=F32)


def _dot_tn(a, b):
    return lax.dot_general(a, b, (((0,), (0,)), ((), ())), preferred_element_type=F32)


def _split2(x):
    hi = x.astype(BF16)
    lo = (x - hi.astype(F32)).astype(BF16)
    return hi, lo


def _split3(x):
    p1 = x.astype(BF16)
    r1 = x - p1.astype(F32)
    p2 = r1.astype(BF16)
    p3 = (r1 - p2.astype(F32)).astype(BF16)
    return p1, p2, p3


def _dot3(a, b):
    a_hi, a_lo = _split2(a)
    b_hi, b_lo = _split2(b)
    return _dot(a_hi, b_hi) + _dot(a_lo, b_hi) + _dot(a_hi, b_lo)


def _dot3_nt(a, b):
    a_hi, a_lo = _split2(a)
    b_hi, b_lo = _split2(b)
    return _dot_nt(a_hi, b_hi) + _dot_nt(a_lo, b_hi) + _dot_nt(a_hi, b_lo)


def _pack(x):
    half = x.shape[1] // 2
    lo = lax.bitcast_convert_type(x[:, :half].astype(BF16).astype(F32), U32)
    hi = lax.bitcast_convert_type(x[:, half:].astype(BF16).astype(F32), U32)
    return (lo >> 16) | (hi & jnp.uint32(0xFFFF0000))


def _unpack_f32(words):
    lo = lax.bitcast_convert_type(words << 16, F32)
    hi = lax.bitcast_convert_type(words & jnp.uint32(0xFFFF0000), F32)
    return lo, hi


def _softplus(z):
    return jnp.maximum(z, 0.0) + jnp.log(1.0 + jnp.exp(-jnp.abs(z)))


def _layer_norm(v, g, b):
    mu = jnp.mean(v, axis=-1, keepdims=True)
    d = v - mu
    var = jnp.mean(d * d, axis=-1, keepdims=True)
    return d * lax.rsqrt(var + LN_EPS) * g + b


def _ada_kernel(c_ref, w_ref, b_ref, o_ref):
    c = c_ref[...]
    s = c * jax.nn.sigmoid(c)
    o_ref[...] = _dot3(s, w_ref[...]) + b_ref[...]


def _ada(c_pad, w, b):
    rows, d = c_pad.shape
    n = w.shape[1]
    tn = 1024
    return pl.pallas_call(
        _ada_kernel,
        out_shape=jax.ShapeDtypeStruct((rows, n), F32),
        grid=(n // tn,),
        in_specs=[pl.BlockSpec((rows, d), lambda j: (0, 0)),
                  pl.BlockSpec((d, tn), lambda j: (0, j)),
                  pl.BlockSpec((1, tn), lambda j: (0, j))],
        out_specs=pl.BlockSpec((rows, tn), lambda j: (0, j)),
        compiler_params=_cparams(("arbitrary",)),
        name="ada",
    )(c_pad, w, b)


def _inproj_kernel(x_ref, sc_ref, sh_ref, wtok_ref, wfeat_ref, bf_ref, place_ref,
                   kf_ref, ks_ref, qf_ref, qs_ref, vf_ref, vs_ref, carry_ref, *, t, tk):
    d_slab = N_HEADS * SLAB
    d_val = N_HEADS * HEAD_DIM

    @pl.when(pl.program_id(1) == 0)
    def _():
        carry_ref[...] = jnp.zeros_like(carry_ref)

    u = x_ref[0] * (1.0 + sc_ref[0]) + sh_ref[0]
    ub = u.astype(BF16)

    flog = _dot(ub, wtok_ref[:, 2 * d_slab:]) + bf_ref[...]
    logf = -_softplus(-flog)
    lane = lax.broadcasted_iota(I32, logf.shape, 1)
    logf = jnp.where(lane < N_HEADS, logf, 0.0)
    row = lax.broadcasted_iota(I32, (t, t), 0)
    col = lax.broadcasted_iota(I32, (t, t), 1)
    tri = jnp.where(row >= col, 1.0, 0.0).astype(BF16)
    lf_hi, lf_lo = _split2(logf)
    cum = _dot(tri, lf_hi) + _dot(tri, lf_lo) + carry_ref[...]
    carry_ref[...] = cum[t - 1:t, :]
    n1, n2, n3 = _split3(-LOG2_E * cum)
    extras = _dot(n1, place_ref[0]) + _dot(n2, place_ref[1]) + _dot(n3, place_ref[2])

    kf_ref[0] = (_dot(ub, wtok_ref[:, :d_slab]) + extras).astype(BF16)
    ks_ref[0] = _dot(ub, wtok_ref[:, d_slab:2 * d_slab]).astype(BF16)

    qf = _dot_nt(wfeat_ref[:d_slab, :], ub)
    r = lax.broadcasted_iota(I32, qf.shape, 0) % SLAB
    qf = jnp.where((r >= HEAD_DIM) & (r < HEAD_DIM + 3), 1.0, qf)
    qf_ref[0] = qf.astype(BF16)
    qs_ref[0] = _dot_nt(wfeat_ref[d_slab:2 * d_slab, :], ub).astype(BF16)
    vf = _dot_nt(wfeat_ref[2 * d_slab:2 * d_slab + d_val, :], ub).astype(BF16)
    vs = _dot_nt(wfeat_ref[2 * d_slab + d_val:, :], ub).astype(BF16)
    for c in range(t // tk):
        vf_ref[0, c] = vf[:, c * tk:(c + 1) * tk]
        vs_ref[0, c] = vs[:, c * tk:(c + 1) * tk]


def _inproj(x, sc, sh, wtok, wfeat, bf_row, place, *, t, tk):
    b, s, d = x.shape
    d_slab = N_HEADS * SLAB
    d_val = N_HEADS * HEAD_DIM
    nt = s // t
    out_shape = (
        jax.ShapeDtypeStruct((b, s, d_slab), BF16),
        jax.ShapeDtypeStruct((b, s, d_slab), BF16),
        jax.ShapeDtypeStruct((b, d_slab, s), BF16),
        jax.ShapeDtypeStruct((b, d_slab, s), BF16),
        jax.ShapeDtypeStruct((b, s // tk, d_val, tk), BF16),
        jax.ShapeDtypeStruct((b, s // tk, d_val, tk), BF16),
    )
    tok_spec = pl.BlockSpec((1, t, d_slab), lambda i, j: (i, j, 0))
    feat_spec = pl.BlockSpec((1, d_slab, t), lambda i, j: (i, 0, j))
    val_spec = pl.BlockSpec((1, t // tk, d_val, tk), lambda i, j: (i, j, 0, 0))
    return pl.pallas_call(
        functools.partial(_inproj_kernel, t=t, tk=tk),
        out_shape=out_shape,
        grid=(b, nt),
        in_specs=[pl.BlockSpec((1, t, d), lambda i, j: (i, j, 0)),
                  pl.BlockSpec((1, 1, d), lambda i, j: (i, 0, 0)),
                  pl.BlockSpec((1, 1, d), lambda i, j: (i, 0, 0)),
                  pl.BlockSpec(wtok.shape, lambda i, j: (0, 0)),
                  pl.BlockSpec(wfeat.shape, lambda i, j: (0, 0)),
                  pl.BlockSpec((1, SLAB), lambda i, j: (0, 0)),
                  pl.BlockSpec(place.shape, lambda i, j: (0, 0, 0))],
        out_specs=(tok_spec, tok_spec, feat_spec, feat_spec, val_spec, val_spec),
        scratch_shapes=[pltpu.VMEM((1, SLAB), F32)],
        compiler_params=_cparams(("arbitrary", "arbitrary")),
        name="inproj",
    )(x, sc, sh, wtok, wfeat, bf_row, place)


def _head_norm_store(acc, g_ref, o_ref, h):
    rows = slice(h * HEAD_DIM, (h + 1) * HEAD_DIM)
    ms = jnp.mean(acc * acc, axis=0, keepdims=True)
    o_ref[0, rows, :] = (acc * lax.rsqrt(ms + RMS_EPS) * g_ref[rows, :]).astype(BF16)


def _fox_kernel(q_ref, k_ref, v_ref, g_ref, o_ref, *, tq, heads):
    qi = pl.program_id(2)

    def tile(kb, carry, masked):
        start = pl.multiple_of(kb * tq, tq)
        k_all = k_ref[0, pl.ds(start, tq), :]
        v_all = v_ref[0, kb]
        scores = [_dot(k_all[:, h * SLAB:(h + 1) * SLAB], q_ref[0, h * SLAB:(h + 1) * SLAB, :])
                  for h in range(heads)]
        out = []
        for h in range(heads):
            m, l, acc = carry[h]
            s = scores[h]
            if masked:
                kid = lax.broadcasted_iota(I32, s.shape, 0)
                qid = lax.broadcasted_iota(I32, s.shape, 1)
                s = jnp.where(kid <= qid, s, NEG_BIG)
            m_new = jnp.maximum(m, jnp.max(s, axis=0, keepdims=True))
            p = jnp.exp2(s - m_new)
            alpha = jnp.exp2(m - m_new)
            l = alpha * l + jnp.sum(p, axis=0, keepdims=True)
            v = v_all[h * HEAD_DIM:(h + 1) * HEAD_DIM, :]
            acc = alpha * acc + _dot(v, p.astype(BF16))
            out.append((m_new, l, acc))
        return tuple(out)

    init = tuple((jnp.full((1, tq), NEG_BIG, F32), jnp.zeros((1, tq), F32),
                  jnp.zeros((HEAD_DIM, tq), F32)) for _ in range(heads))
    carry = lax.fori_loop(0, qi, lambda kb, c: tile(kb, c, False), init)
    final = tile(qi, carry, True)
    for h in range(heads):
        _, l, acc = final[h]
        _head_norm_store(acc / l, g_ref, o_ref, h)


def _sb_kernel(q_ref, k_ref, v_ref, g_ref, o_ref, *, tq, heads):
    qi = pl.program_id(2)
    cb = SB_CHUNK
    nchunk = tq // cb
    r = lax.broadcasted_iota(I32, (cb + 16, 2 * cb), 0)
    c = lax.broadcasted_iota(I32, (cb + 16, 2 * cb), 1) % cb
    uu = jnp.where(((r < cb) & (c > r)) | (r == cb), 1.0, 0.0).astype(BF16)
    row = lax.broadcasted_iota(I32, (tq, tq), 0)
    col = lax.broadcasted_iota(I32, (tq, tq), 1)
    past = row < col

    def tile(kb, carry, masked):
        start = pl.multiple_of(kb * tq, tq)
        k_all = k_ref[0, pl.ds(start, tq), :]
        v_all = v_ref[0, kb]
        zs = [_dot(k_all[:, h * SLAB:(h + 1) * SLAB], q_ref[0, h * SLAB:(h + 1) * SLAB, :])
              for h in range(heads)]
        lbs, exts = [], []
        for h in range(heads):
            z = zs[h]
            sp = jnp.maximum(z, 0.0) + jnp.log2(1.0 + jnp.exp2(-jnp.abs(z)))
            lbs.append(z - sp)
            if masked:
                sp = jnp.where(past, sp, 0.0)
            hi, lo = _split2(sp)
            exts.append([_dot(uu, jnp.concatenate([hi[ci * cb:(ci + 1) * cb, :],
                                                   lo[ci * cb:(ci + 1) * cb, :]], axis=0))
                         for ci in range(nchunk)])
        out = []
        for h in range(heads):
            run, acc = carry[h]
            chunks = [None] * nchunk
            for ci in reversed(range(nchunk)):
                rows = slice(ci * cb, (ci + 1) * cb)
                a = jnp.exp2(lbs[h][rows, :] - exts[h][ci][:cb, :] - run)
                if masked:
                    a = jnp.where(past[rows, :], a, 0.0)
                chunks[ci] = a.astype(BF16)
                run = run + exts[h][ci][cb:cb + 1, :]
            v = v_all[h * HEAD_DIM:(h + 1) * HEAD_DIM, :]
            acc = acc + _dot(v, jnp.concatenate(chunks, axis=0))
            out.append((run, acc))
        return tuple(out)

    init = tuple((jnp.zeros((1, tq), F32), jnp.zeros((HEAD_DIM, tq), F32)) for _ in range(heads))
    carry = tile(qi, init, True)
    final = lax.fori_loop(0, qi, lambda i, cr: tile(qi - 1 - i, cr, False), carry)
    for h in range(heads):
        _head_norm_store(final[h][1], g_ref, o_ref, h)


def _attention(kernel, qt, k, vt, g_col, *, tq, heads, name):
    b, d_slab, s = qt.shape
    nk = vt.shape[1]
    d_val = N_HEADS * HEAD_DIM
    return pl.pallas_call(
        functools.partial(kernel, tq=tq, heads=heads),
        out_shape=jax.ShapeDtypeStruct((b, d_val, s), BF16),
        grid=(b, N_HEADS // heads, s // tq),
        in_specs=[pl.BlockSpec((1, heads * SLAB, tq), lambda i, h, j: (i, h, j)),
                  pl.BlockSpec((1, s, heads * SLAB), lambda i, h, j: (i, 0, h)),
                  pl.BlockSpec((1, nk, heads * HEAD_DIM, tq), lambda i, h, j: (i, 0, h, 0)),
                  pl.BlockSpec((heads * HEAD_DIM, 1), lambda i, h, j: (h, 0))],
        out_specs=pl.BlockSpec((1, heads * HEAD_DIM, tq), lambda i, h, j: (i, h, j)),
        compiler_params=_cparams(("arbitrary", "arbitrary", "arbitrary")),
        name=name,
    )(qt, k, vt, g_col)


def _outproj_kernel(of_ref, os_ref, x_ref, gate_ref, sc_ref, sh_ref, wa_ref, wb_ref,
                    g_ref, b_ref, wr_ref, x1_ref, up_ref, lg_ref, *, alpha):
    d = x_ref.shape[2]
    mix = _dot_tn(of_ref[0], wa_ref[...]) + _dot_tn(os_ref[0], wb_ref[...])
    x1 = _layer_norm(alpha * x_ref[0] + (1.0 + gate_ref[0]) * mix, g_ref[...], b_ref[...])
    x1_ref[...] = x1
    u2 = x1 * (1.0 + sc_ref[0]) + sh_ref[0]
    up_ref[...] = _pack(u2)
    lg_ref[...] = _dot3_nt(wr_ref[...], u2)


def _outproj(of_t, os_t, x, gate, sc, sh, wa, wb, g, bb, wr_t, *, t, alpha):
    b, s, d = x.shape
    n = b * s
    nt = s // t
    e = wr_t.shape[0]
    d_val = of_t.shape[1]
    mod_spec = pl.BlockSpec((1, 1, d), lambda i, j: (i, 0, 0))
    return pl.pallas_call(
        functools.partial(_outproj_kernel, alpha=alpha),
        out_shape=(jax.ShapeDtypeStruct((n, d), F32),
                   jax.ShapeDtypeStruct((n, d // 2), U32),
                   jax.ShapeDtypeStruct((e, n), F32)),
        grid=(b, nt),
        in_specs=[pl.BlockSpec((1, d_val, t), lambda i, j: (i, 0, j)),
                  pl.BlockSpec((1, d_val, t), lambda i, j: (i, 0, j)),
                  pl.BlockSpec((1, t, d), lambda i, j: (i, j, 0)),
                  mod_spec, mod_spec, mod_spec,
                  pl.BlockSpec(wa.shape, lambda i, j: (0, 0)),
                  pl.BlockSpec(wb.shape, lambda i, j: (0, 0)),
                  pl.BlockSpec((1, d), lambda i, j: (0, 0)),
                  pl.BlockSpec((1, d), lambda i, j: (0, 0)),
                  pl.BlockSpec(wr_t.shape, lambda i, j: (0, 0))],
        out_specs=(pl.BlockSpec((t, d), lambda i, j: (i * nt + j, 0)),
                   pl.BlockSpec((t, d // 2), lambda i, j: (i * nt + j, 0)),
                   pl.BlockSpec((e, t), lambda i, j: (0, i * nt + j))),
        compiler_params=_cparams(("arbitrary", "arbitrary")),
        name="outproj",
    )(of_t, os_t, x, gate, sc, sh, wa, wb, g, bb, wr_t)


def _route_kernel(lg_ref, bias_ref, idx_ref, rank_ref, w_ref, cnt_ref, *, t):
    @pl.when(pl.program_id(0) == 0)
    def _():
        cnt_ref[...] = jnp.zeros_like(cnt_ref)

    scores = jax.nn.sigmoid(lg_ref[...])
    sel = scores + bias_ref[...]
    neg_inf = -jnp.inf

    gscore = []
    for gi in range(N_GROUPS):
        blk = sel[gi * GROUP_SIZE:(gi + 1) * GROUP_SIZE, :]
        m1 = jnp.max(blk, axis=0, keepdims=True)
        is_max = blk == m1
        n_max = jnp.sum(jnp.where(is_max, 1.0, 0.0), axis=0, keepdims=True)
        m2 = jnp.max(jnp.where(is_max, neg_inf, blk), axis=0, keepdims=True)
        gscore.append(m1 + jnp.where(n_max >= 2.0, m1, m2))

    parts = []
    for gi in range(N_GROUPS):
        beaten = jnp.zeros_like(gscore[gi])
        for gj in range(N_GROUPS):
            if gj == gi:
                continue
            wins = (gscore[gj] > gscore[gi]) if gj > gi else (gscore[gj] >= gscore[gi])
            beaten = beaten + jnp.where(wins, 1.0, 0.0)
        keep = beaten < float(TOP_GROUPS)
        blk = sel[gi * GROUP_SIZE:(gi + 1) * GROUP_SIZE, :]
        parts.append(jnp.where(keep, blk, neg_inf))
    cand = jnp.concatenate(parts, axis=0)

    eidx = lax.broadcasted_iota(I32, cand.shape, 0).astype(F32)
    chosen = jnp.zeros(cand.shape, F32)
    idx_rows, w_rows = [], []
    for _ in range(TOP_K):
        mx = jnp.max(cand, axis=0, keepdims=True)
        idx = jnp.min(jnp.where(cand == mx, eidx, float(N_EXPERTS)), axis=0, keepdims=True)
        hit = eidx == idx
        w_rows.append(jnp.sum(jnp.where(hit, scores, 0.0), axis=0, keepdims=True))
        idx_rows.append(idx)
        cand = jnp.where(hit, neg_inf, cand)
        chosen = jnp.where(hit, 1.0, chosen)

    w_sum = w_rows[0]
    for wk in w_rows[1:]:
        w_sum = w_sum + wk

    r = lax.broadcasted_iota(I32, (t, t), 0)
    c = lax.broadcasted_iota(I32, (t, t), 1)
    before = jnp.where(r < c, 1.0, 0.0).astype(BF16)
    prefix = _dot(chosen.astype(BF16), before) + cnt_ref[...]
    for k in range(TOP_K):
        hit = eidx == idx_rows[k]
        rank = jnp.sum(jnp.where(hit, prefix, 0.0), axis=0, keepdims=True)
        idx_ref[k:k + 1, :] = idx_rows[k].astype(I32)
        rank_ref[k:k + 1, :] = rank.astype(I32)
        w_ref[k:k + 1, :] = w_rows[k] / w_sum * ROUTED_SCALE
    cnt_ref[...] = cnt_ref[...] + jnp.sum(chosen, axis=1, keepdims=True)


def _route(logits_t, bias_col, *, t):
    e, n = logits_t.shape
    row_spec = pl.BlockSpec((TOP_K, t), lambda i: (0, i))
    return pl.pallas_call(
        functools.partial(_route_kernel, t=t),
        out_shape=(jax.ShapeDtypeStruct((TOP_K, n), I32),
                   jax.ShapeDtypeStruct((TOP_K, n), I32),
                   jax.ShapeDtypeStruct((TOP_K, n), F32),
                   jax.ShapeDtypeStruct((e, 1), F32)),
        grid=(n // t,),
        in_specs=[pl.BlockSpec((e, t), lambda i: (0, i)),
                  pl.BlockSpec((e, 1), lambda i: (0, 0))],
        out_specs=(row_spec, row_spec, row_spec, pl.BlockSpec((e, 1), lambda i: (0, 0))),
        compiler_params=_cparams(("arbitrary",)),
        name="route",
    )(logits_t, bias_col)


def _dest_kernel(cnt_ref, idx_ref, rank_ref, dest_ref, be_ref, nb_ref, pstart_ref, *, n_blocks):
    e = cnt_ref.shape[0]

    @pl.when(pl.program_id(0) == 0)
    def _():
        nblk = jnp.floor((cnt_ref[...] + float(MOE_BLOCK - 1)) * (1.0 / MOE_BLOCK))
        hi = jnp.floor(nblk * (1.0 / 32.0))
        lo = nblk - 32.0 * hi
        r = lax.broadcasted_iota(I32, (e, e), 0)
        c = lax.broadcasted_iota(I32, (e, e), 1)
        below = jnp.where(c < r, 1.0, 0.0).astype(BF16)
        hi_b = jnp.broadcast_to(hi, (e, SLAB)).astype(BF16)
        lo_b = jnp.broadcast_to(lo, (e, SLAB)).astype(BF16)
        bstart = 32.0 * _dot(below, hi_b) + _dot(below, lo_b)
        pstart_ref[...] = bstart[:, 0:1] * float(MOE_BLOCK)
        bend = bstart[:, 0:1] + nblk
        bid = lax.broadcasted_iota(I32, (e, n_blocks), 1).astype(F32)
        owner = jnp.sum(jnp.where(bend <= bid, 1.0, 0.0), axis=0, keepdims=True)
        be_ref[...] = jnp.minimum(owner, float(e - 1)).astype(I32)
        nb_ref[...] = jnp.broadcast_to(jnp.max(bend, axis=0, keepdims=True), nb_ref.shape).astype(I32)

    eidx = lax.broadcasted_iota(I32, (e, idx_ref.shape[1]), 0)
    for k in range(TOP_K):
        hit = eidx == idx_ref[k:k + 1, :]
        base = jnp.sum(jnp.where(hit, pstart_ref[...], 0.0), axis=0, keepdims=True)
        dest_ref[k:k + 1, :] = base.astype(I32) + rank_ref[k:k + 1, :]


def _dest(counts, idx_t, rank_t, *, t, n_blocks):
    e = counts.shape[0]
    n = idx_t.shape[1]
    row_spec = pl.BlockSpec((TOP_K, t), lambda i: (0, i))
    return pl.pallas_call(
        functools.partial(_dest_kernel, n_blocks=n_blocks),
        out_shape=(jax.ShapeDtypeStruct((TOP_K, n), I32),
                   jax.ShapeDtypeStruct((1, n_blocks), I32),
                   jax.ShapeDtypeStruct((1, SLAB), I32)),
        grid=(n // t,),
        in_specs=[pl.BlockSpec((e, 1), lambda i: (0, 0)), row_spec, row_spec],
        out_specs=(row_spec,
                   pl.BlockSpec((1, n_blocks), lambda i: (0, 0)),
                   pl.BlockSpec((1, SLAB), lambda i: (0, 0))),
        scratch_shapes=[pltpu.VMEM((e, 1), F32)],
        compiler_params=_cparams(("arbitrary",)),
        name="dest",
    )(counts, idx_t, rank_t)


def _row_copy(src_ref, src_row, dst_ref, dst_row, sem):
    return pltpu.make_async_copy(src_ref.at[pl.ds(src_row, 1)], dst_ref.at[pl.ds(dst_row, 1)], sem)


def _dispatch_kernel(dest_ref, up_ref, xs_in_ref, xs_ref, sem, *, t):
    del xs_in_ref

    def issue(j, _):
        for k in range(TOP_K):
            _row_copy(up_ref, j, xs_ref, dest_ref[k * t + j], sem).start()
        return 0

    lax.fori_loop(0, t, issue, 0)

    def drain(j, _):
        for k in range(TOP_K):
            _row_copy(up_ref, j, xs_ref, dest_ref[k * t + j], sem).wait()
        return 0

    lax.fori_loop(0, t, drain, 0)


def _dispatch(dest_flat, up, xs_zero, *, t):
    n = up.shape[0]
    return pl.pallas_call(
        functools.partial(_dispatch_kernel, t=t),
        out_shape=jax.ShapeDtypeStruct(xs_zero.shape, xs_zero.dtype),
        grid=(n // t,),
        in_specs=[pl.BlockSpec((TOP_K * t,), lambda i: (i,), memory_space=pltpu.SMEM),
                  pl.BlockSpec((t, up.shape[1]), lambda i: (i, 0)),
                  pl.BlockSpec(memory_space=pl.ANY)],
        out_specs=pl.BlockSpec(memory_space=pl.ANY),
        scratch_shapes=[pltpu.SemaphoreType.DMA(())],
        input_output_aliases={2: 0},
        compiler_params=_cparams(("arbitrary",)),
        name="dispatch",
    )(dest_flat, up, xs_zero)


def _unpack(words):
    lo, hi = _unpack_f32(words)
    return lo.astype(BF16), hi.astype(BF16)


def _expert_kernel(be_ref, nb_ref, xs_ref, wg_ref, wu_ref, wd_ref, ys_ref, wgu_s, wd_s):
    blk = pl.program_id(0)
    d_half = xs_ref.shape[1]
    d_exp = wg_ref.shape[2]
    prev = be_ref[jnp.maximum(blk - 1, 0)]
    fresh = (blk == 0) | (prev != be_ref[blk])
    active = blk < nb_ref[0]

    @pl.when(active & fresh)
    def _():
        wgu_s[:, :d_exp] = wg_ref[0].astype(BF16)
        wgu_s[:, d_exp:] = wu_ref[0].astype(BF16)
        wd_s[...] = wd_ref[0].astype(BF16)

    @pl.when(active)
    def _():
        rows = MOE_BLOCK // MOE_SPLIT
        for part in range(MOE_SPLIT):
            sl = slice(part * rows, (part + 1) * rows)
            lo, hi = _unpack(xs_ref[sl, :])
            gu = _dot(lo, wgu_s[:d_half, :]) + _dot(hi, wgu_s[d_half:, :])
            gate = gu[:, :d_exp]
            h = gate * jax.nn.sigmoid(gate) * gu[:, d_exp:]
            ys_ref[sl, :] = _pack(_dot(h.astype(BF16), wd_s[...]))

    @pl.when(jnp.logical_not(active))
    def _():
        ys_ref[...] = jnp.zeros_like(ys_ref)


def _experts(blk_expert, n_active, xs, wg, wu, wd):
    cap, d_half = xs.shape
    d = 2 * d_half
    d_exp = wg.shape[2]
    n_blocks = cap // MOE_BLOCK
    grid_spec = pltpu.PrefetchScalarGridSpec(
        num_scalar_prefetch=2,
        grid=(n_blocks,),
        in_specs=[pl.BlockSpec((MOE_BLOCK, d_half), lambda i, be, nb: (jnp.minimum(i, jnp.maximum(nb[0] - 1, 0)), 0)),
                  pl.BlockSpec((1, d, d_exp), lambda i, be, nb: (be[i], 0, 0)),
                  pl.BlockSpec((1, d, d_exp), lambda i, be, nb: (be[i], 0, 0)),
                  pl.BlockSpec((1, d_exp, d), lambda i, be, nb: (be[i], 0, 0))],
        out_specs=pl.BlockSpec((MOE_BLOCK, d_half), lambda i, be, nb: (i, 0)),
        scratch_shapes=[pltpu.VMEM((d, 2 * d_exp), BF16), pltpu.VMEM((d_exp, d), BF16)],
    )
    return pl.pallas_call(
        _expert_kernel,
        out_shape=jax.ShapeDtypeStruct((cap, d_half), U32),
        grid_spec=grid_spec,
        compiler_params=_cparams(("arbitrary",)),
        name="experts",
    )(blk_expert, n_active, xs, wg, wu, wd)


def _combine_kernel(dest_ref, ys_ref, w_ref, x1_ref, gate_ref, sc_ref, sh_ref,
                    wgu_ref, wd_ref, g_ref, b_ref, o_ref, buf, sem, *, t, alpha):
    def issue(j, _):
        for k in range(TOP_K):
            _row_copy(ys_ref, dest_ref[k * t + j], buf.at[k], j, sem).start()
        return 0

    lax.fori_loop(0, t, issue, 0)

    x1 = x1_ref[...]
    ub = (x1 * (1.0 + sc_ref[0]) + sh_ref[0]).astype(BF16)
    d_sh = wd_ref.shape[0]
    gu = _dot(ub, wgu_ref[...])
    gate = gu[:, :d_sh]
    h = gate * jax.nn.sigmoid(gate) * gu[:, d_sh:]
    ffn = _dot(h.astype(BF16), wd_ref[...])

    def drain(j, _):
        for k in range(TOP_K):
            _row_copy(ys_ref, dest_ref[k * t + j], buf.at[k], j, sem).wait()
        return 0

    lax.fori_loop(0, t, drain, 0)

    w = w_ref[...]
    d_half = buf.shape[2]
    lo_sum = ffn[:, :d_half]
    hi_sum = ffn[:, d_half:]
    for k in range(TOP_K):
        lo, hi = _unpack_f32(buf[k])
        lo_sum = lo_sum + w[:, k:k + 1] * lo
        hi_sum = hi_sum + w[:, k:k + 1] * hi
    ffn = jnp.concatenate([lo_sum, hi_sum], axis=1)
    o_ref[...] = _layer_norm(alpha * x1 + (1.0 + gate_ref[0]) * ffn, g_ref[...], b_ref[...])


def _combine(dest_flat, ys, w_tok, x1, gate, sc, sh, wgu, wd, g, bb, *, t, seq, alpha):
    n, d = x1.shape
    per_batch = seq // t
    mod_spec = pl.BlockSpec((1, 1, d), lambda i: (i // per_batch, 0, 0))
    return pl.pallas_call(
        functools.partial(_combine_kernel, t=t, alpha=alpha),
        out_shape=jax.ShapeDtypeStruct((n, d), F32),
        grid=(n // t,),
        in_specs=[pl.BlockSpec((TOP_K * t,), lambda i: (i,), memory_space=pltpu.SMEM),
                  pl.BlockSpec(memory_space=pl.ANY),
                  pl.BlockSpec((t, TOP_K), lambda i: (i, 0)),
                  pl.BlockSpec((t, d), lambda i: (i, 0)),
                  mod_spec, mod_spec, mod_spec,
                  pl.BlockSpec(wgu.shape, lambda i: (0, 0)),
                  pl.BlockSpec(wd.shape, lambda i: (0, 0)),
                  pl.BlockSpec((1, d), lambda i: (0, 0)),
                  pl.BlockSpec((1, d), lambda i: (0, 0))],
        out_specs=pl.BlockSpec((t, d), lambda i: (i, 0)),
        scratch_shapes=[pltpu.VMEM((TOP_K, t, d // 2), U32), pltpu.SemaphoreType.DMA(())],
        compiler_params=_cparams(("arbitrary",)),
        name="combine",
    )(dest_flat, ys, w_tok, x1, gate, sc, sh, wgu, wd, g, bb)


def _slab_cols(w, scale=1.0):
    d = w.shape[0]
    w = (w * scale).reshape(d, N_HEADS, HEAD_DIM)
    w = jnp.pad(w, ((0, 0), (0, 0), (0, SLAB - HEAD_DIM)))
    return w.reshape(d, N_HEADS * SLAB)


def _placement():
    h = jnp.arange(SLAB)[:, None]
    c = jnp.arange(N_HEADS * SLAB)[None, :]
    mats = [((h < N_HEADS) & (c == h * SLAB + HEAD_DIM + p)) for p in range(3)]
    return jnp.stack(mats).astype(BF16)


def _tile_major(rows, t):
    k, n = rows.shape
    return rows.reshape(k, n // t, t).transpose(1, 0, 2).reshape(-1)


def _layer(x, c_pad, w_ada, b_ada, w_in, b_f, fox_g, sb_g, w_out, ln1_g, ln1_b,
           w_router, router_bias, w_gate_e, w_up_e, w_down_e,
           w_gate_sh, w_up_sh, w_down_sh, ln2_g, ln2_b, *, alpha,
           t_proj=512, t_attn=512, attn_heads=4, t_route=512, t_moe=128):
    b, s, d = x.shape
    n = b * s
    d_val = N_HEADS * HEAD_DIM
    scale = HEAD_DIM ** -0.5 * LOG2_E

    ada = _ada(c_pad, w_ada, b_ada.reshape(1, -1))[:b]
    shift1, scale1, gate1, shift2, scale2, gate2 = [m[:, None, :] for m in jnp.split(ada, 6, axis=-1)]

    q_f, k_f, v_f, q_s, k_s, v_s, w_f = jnp.split(
        w_in, [d_val, 2 * d_val, 3 * d_val, 4 * d_val, 5 * d_val, 6 * d_val], axis=1)
    w_f = jnp.pad(w_f, ((0, 0), (0, SLAB - N_HEADS)))
    wtok = jnp.concatenate([_slab_cols(k_f), _slab_cols(k_s), w_f], axis=1).astype(BF16)
    wfeat = jnp.concatenate([_slab_cols(q_f, scale), _slab_cols(q_s, scale), v_f, v_s], axis=1).T.astype(BF16)
    bf_row = jnp.pad(b_f, (0, SLAB - N_HEADS)).reshape(1, SLAB)

    kf, ks, qf_t, qs_t, vf_t, vs_t = _inproj(x, scale1, shift1, wtok, wfeat, bf_row, _placement(),
                                             t=t_proj, tk=t_attn)
    of_t = _attention(_fox_kernel, qf_t, kf, vf_t, fox_g.reshape(-1, 1),
                      tq=t_attn, heads=attn_heads, name="fox")
    os_t = _attention(_sb_kernel, qs_t, ks, vs_t, sb_g.reshape(-1, 1),
                      tq=t_attn, heads=attn_heads, name="sb")

    x1, u_packed, logits_t = _outproj(
        of_t, os_t, x, gate1, scale2, shift2,
        w_out[:d_val].astype(BF16), w_out[d_val:].astype(BF16),
        ln1_g.reshape(1, d), ln1_b.reshape(1, d), w_router.T, t=t_proj, alpha=alpha)

    idx_t, rank_t, w_t, counts = _route(logits_t, router_bias.reshape(-1, 1), t=t_route)
    n_blocks = -(-(n * TOP_K) // MOE_BLOCK) + N_EXPERTS
    dest_t, blk_expert, n_active = _dest(counts, idx_t, rank_t, t=t_route, n_blocks=n_blocks)
    dest_flat = _tile_major(dest_t, t_moe)

    xs = _dispatch(dest_flat, u_packed, jnp.zeros((n_blocks * MOE_BLOCK, d // 2), U32), t=t_moe)
    ys = _experts(blk_expert.reshape(-1), n_active.reshape(-1)[:1], xs, w_gate_e, w_up_e, w_down_e)

    wgu_sh = jnp.concatenate([w_gate_sh, w_up_sh], axis=1).astype(BF16)
    out = _combine(dest_flat, ys, w_t.T, x1, gate2, scale2, shift2, wgu_sh, w_down_sh.astype(BF16),
                   ln2_g.reshape(1, d), ln2_b.reshape(1, d), t=t_moe, seq=s, alpha=alpha)
    return out.reshape(b, s, d)


def kernel(x, c, w_ada, b_ada, w_in, b_f, fox_norm_g, sb_norm_g, w_out, ln1_g, ln1_b, w_router, router_bias, w_gate_e, w_up_e, w_down_e, w_gate_sh, w_up_sh, w_down_sh, ln2_g, ln2_b):
    depth = w_ada.shape[0]
    alpha = (2.0 * depth) ** 0.25
    c_pad = jnp.pad(c, ((0, (-c.shape[0]) % 8), (0, 0)))
    for l in range(depth):
        x = _layer(x, c_pad, w_ada[l], b_ada[l], w_in[l], b_f[l], fox_norm_g[l], sb_norm_g[l],
                   w_out[l], ln1_g[l], ln1_b[l], w_router[l], router_bias[l],
                   w_gate_e[l], w_up_e[l], w_down_e[l], w_gate_sh[l], w_up_sh[l], w_down_sh[l],
                   ln2_g[l], ln2_b[l], alpha=alpha)
    return x
```

```python
import functools

import jax
import jax.numpy as jnp
from jax import lax
from jax.experimental import pallas as pl
from jax.experimental.pallas import tpu as pltpu

F32 = jnp.float32
BF16 = jnp.bfloat16
I32 = jnp.int32
U32 = jnp.uint32

HEAD_DIM = 64
N_HEADS = 8
SLAB = 128
N_EXPERTS = 256
TOP_K = 8
N_GROUPS = 8
TOP_GROUPS = 4
GROUP_SIZE = N_EXPERTS // N_GROUPS
ROUTED_SCALE = 2.5
MOE_BLOCK = 256
MOE_SPLIT = 2
SB_CHUNK = 128
LN_EPS = 1e-5
RMS_EPS = 1e-6
NEG_BIG = -1e30
LOG2_E = 1.4426950408889634

VMEM_LIMIT = 56 * 1024 * 1024


def _cparams(sem):
    return pltpu.CompilerParams(dimension_semantics=sem, vmem_limit_bytes=VMEM_LIMIT)


def _dot(a, b):
    return jnp.dot(a, b, preferred_element_type=F32)


def _dot_nt(a, b):
    return lax.dot_general(a, b, (((1,), (1,)), ((), ())), preferred_element_type=F32)


def _dot_tn(a, b):
    return lax.dot_general(a, b, (((0,), (0,)), ((), ())), preferred_element_type=F32)


def _split2(x):
    hi = x.astype(BF16)
    lo = (x - hi.astype(F32)).astype(BF16)
    return hi, lo


def _split3(x):
    p1 = x.astype(BF16)
    r1 = x - p1.astype(F32)
    p2 = r1.astype(BF16)
    p3 = (r1 - p2.astype(F32)).astype(BF16)
    return p1, p2, p3


def _dot3(a, b):
    a_hi, a_lo = _split2(a)
    b_hi, b_lo = _split2(b)
    return _dot(a_hi, b_hi) + _dot(a_lo, b_hi) + _dot(a_hi, b_lo)


def _dot3_nt(a, b):
    a_hi, a_lo = _split2(a)
    b_hi, b_lo = _split2(b)
    return _dot_nt(a_hi, b_hi) + _dot_nt(a_lo, b_hi) + _dot_nt(a_hi, b_lo)


def _pack(x):
    half = x.shape[1] // 2
    lo = lax.bitcast_convert_type(x[:, :half].astype(BF16).astype(F32), U32)
    hi = lax.bitcast_convert_type(x[:, half:].astype(BF16).astype(F32), U32)
    return (lo >> 16) | (hi & jnp.uint32(0xFFFF0000))


def _unpack_f32(words):
    lo = lax.bitcast_convert_type(words << 16, F32)
    hi = lax.bitcast_convert_type(words & jnp.uint32(0xFFFF0000), F32)
    return lo, hi


def _softplus(z):
    return jnp.maximum(z, 0.0) + jnp.log(1.0 + jnp.exp(-jnp.abs(z)))


def _layer_norm(v, g, b):
    mu = jnp.mean(v, axis=-1, keepdims=True)
    d = v - mu
    var = jnp.mean(d * d, axis=-1, keepdims=True)
    return d * lax.rsqrt(var + LN_EPS) * g + b


def _ada_kernel(c_ref, w_ref, b_ref, o_ref):
    c = c_ref[...]
    s = c * jax.nn.sigmoid(c)
    o_ref[...] = _dot3(s, w_ref[...]) + b_ref[...]


def _ada(c_pad, w, b):
    rows, d = c_pad.shape
    n = w.shape[1]
    tn = 1024
    return pl.pallas_call(
        _ada_kernel,
        out_shape=jax.ShapeDtypeStruct((rows, n), F32),
        grid=(n // tn,),
        in_specs=[pl.BlockSpec((rows, d), lambda j: (0, 0)),
                  pl.BlockSpec((d, tn), lambda j: (0, j)),
                  pl.BlockSpec((1, tn), lambda j: (0, j))],
        out_specs=pl.BlockSpec((rows, tn), lambda j: (0, j)),
        compiler_params=_cparams(("arbitrary",)),
        name="ada",
    )(c_pad, w, b)


def _inproj_kernel(x_ref, sc_ref, sh_ref, wtok_ref, wfeat_ref, bf_ref, place_ref,
                   kf_ref, ks_ref, qf_ref, qs_ref, vf_ref, vs_ref, carry_ref, *, t, tk):
    d_slab = N_HEADS * SLAB
    d_val = N_HEADS * HEAD_DIM

    @pl.when(pl.program_id(1) == 0)
    def _():
        carry_ref[...] = jnp.zeros_like(carry_ref)

    u = x_ref[0] * (1.0 + sc_ref[0]) + sh_ref[0]
    ub = u.astype(BF16)

    flog = _dot(ub, wtok_ref[:, 2 * d_slab:]) + bf_ref[...]
    logf = -_softplus(-flog)
    lane = lax.broadcasted_iota(I32, logf.shape, 1)
    logf = jnp.where(lane < N_HEADS, logf, 0.0)
    row = lax.broadcasted_iota(I32, (t, t), 0)
    col = lax.broadcasted_iota(I32, (t, t), 1)
    tri = jnp.where(row >= col, 1.0, 0.0).astype(BF16)
    lf_hi, lf_lo = _split2(logf)
    cum = _dot(tri, lf_hi) + _dot(tri, lf_lo) + carry_ref[...]
    carry_ref[...] = cum[t - 1:t, :]
    n1, n2, n3 = _split3(-LOG2_E * cum)
    extras = _dot(n1, place_ref[0]) + _dot(n2, place_ref[1]) + _dot(n3, place_ref[2])

    kf_ref[0] = (_dot(ub, wtok_ref[:, :d_slab]) + extras).astype(BF16)
    ks_ref[0] = _dot(ub, wtok_ref[:, d_slab:2 * d_slab]).astype(BF16)

    qf = _dot_nt(wfeat_ref[:d_slab, :], ub)
    r = lax.broadcasted_iota(I32, qf.shape, 0) % SLAB
    qf = jnp.where((r >= HEAD_DIM) & (r < HEAD_DIM + 3), 1.0, qf)
    qf_ref[0] = qf.astype(BF16)
    qs_ref[0] = _dot_nt(wfeat_ref[d_slab:2 * d_slab, :], ub).astype(BF16)
    vf = _dot_nt(wfeat_ref[2 * d_slab:2 * d_slab + d_val, :], ub).astype(BF16)
    vs = _dot_nt(wfeat_ref[2 * d_slab + d_val:, :], ub).astype(BF16)
    for c in range(t // tk):
        vf_ref[0, c] = vf[:, c * tk:(c + 1) * tk]
        vs_ref[0, c] = vs[:, c * tk:(c + 1) * tk]


def _inproj(x, sc, sh, wtok, wfeat, bf_row, place, *, t, tk):
    b, s, d = x.shape
    d_slab = N_HEADS * SLAB
    d_val = N_HEADS * HEAD_DIM
    nt = s // t
    out_shape = (
        jax.ShapeDtypeStruct((b, s, d_slab), BF16),
        jax.ShapeDtypeStruct((b, s, d_slab), BF16),
        jax.ShapeDtypeStruct((b, d_slab, s), BF16),
        jax.ShapeDtypeStruct((b, d_slab, s), BF16),
        jax.ShapeDtypeStruct((b, s // tk, d_val, tk), BF16),
        jax.ShapeDtypeStruct((b, s // tk, d_val, tk), BF16),
    )
    tok_spec = pl.BlockSpec((1, t, d_slab), lambda i, j: (i, j, 0))
    feat_spec = pl.BlockSpec((1, d_slab, t), lambda i, j: (i, 0, j))
    val_spec = pl.BlockSpec((1, t // tk, d_val, tk), lambda i, j: (i, j, 0, 0))
    return pl.pallas_call(
        functools.partial(_inproj_kernel, t=t, tk=tk),
        out_shape=out_shape,
        grid=(b, nt),
        in_specs=[pl.BlockSpec((1, t, d), lambda i, j: (i, j, 0)),
                  pl.BlockSpec((1, 1, d), lambda i, j: (i, 0, 0)),
                  pl.BlockSpec((1, 1, d), lambda i, j: (i, 0, 0)),
                  pl.BlockSpec(wtok.shape, lambda i, j: (0, 0)),
                  pl.BlockSpec(wfeat.shape, lambda i, j: (0, 0)),
                  pl.BlockSpec((1, SLAB), lambda i, j: (0, 0)),
                  pl.BlockSpec(place.shape, lambda i, j: (0, 0, 0))],
        out_specs=(tok_spec, tok_spec, feat_spec, feat_spec, val_spec, val_spec),
        scratch_shapes=[pltpu.VMEM((1, SLAB), F32)],
        compiler_params=_cparams(("arbitrary", "arbitrary")),
        name="inproj",
    )(x, sc, sh, wtok, wfeat, bf_row, place)


def _head_norm_store(acc, g_ref, o_ref, h):
    rows = slice(h * HEAD_DIM, (h + 1) * HEAD_DIM)
    ms = jnp.mean(acc * acc, axis=0, keepdims=True)
    o_ref[0, rows, :] = (acc * lax.rsqrt(ms + RMS_EPS) * g_ref[rows, :]).astype(BF16)


def _fox_kernel(q_ref, k_ref, v_ref, g_ref, o_ref, *, tq, heads):
    qi = pl.program_id(2)

    def tile(kb, carry, masked):
        start = pl.multiple_of(kb * tq, tq)
        k_all = k_ref[0, pl.ds(start, tq), :]
        v_all = v_ref[0, kb]
        scores = [_dot(k_all[:, h * SLAB:(h + 1) * SLAB], q_ref[0, h * SLAB:(h + 1) * SLAB, :])
                  for h in range(heads)]
        out = []
        for h in range(heads):
            m, l, acc = carry[h]
            s = scores[h]
            if masked:
                kid = lax.broadcasted_iota(I32, s.shape, 0)
                qid = lax.broadcasted_iota(I32, s.shape, 1)
                s = jnp.where(kid <= qid, s, NEG_BIG)
            m_new = jnp.maximum(m, jnp.max(s, axis=0, keepdims=True))
            p = jnp.exp2(s - m_new)
            alpha = jnp.exp2(m - m_new)
            l = alpha * l + jnp.sum(p, axis=0, keepdims=True)
            v = v_all[h * HEAD_DIM:(h + 1) * HEAD_DIM, :]
            acc = alpha * acc + _dot(v, p.astype(BF16))
            out.append((m_new, l, acc))
        return tuple(out)

    init = tuple((jnp.full((1, tq), NEG_BIG, F32), jnp.zeros((1, tq), F32),
                  jnp.zeros((HEAD_DIM, tq), F32)) for _ in range(heads))
    carry = lax.fori_loop(0, qi, lambda kb, c: tile(kb, c, False), init)
    final = tile(qi, carry, True)
    for h in range(heads):
        _, l, acc = final[h]
        _head_norm_store(acc / l, g_ref, o_ref, h)


def _sb_kernel(q_ref, k_ref, v_ref, g_ref, o_ref, *, tq, heads):
    qi = pl.program_id(2)
    cb = SB_CHUNK
    nchunk = tq // cb
    r = lax.broadcasted_iota(I32, (cb + 16, 2 * cb), 0)
    c = lax.broadcasted_iota(I32, (cb + 16, 2 * cb), 1) % cb
    uu = jnp.where(((r < cb) & (c > r)) | (r == cb), 1.0, 0.0).astype(BF16)
    row = lax.broadcasted_iota(I32, (tq, tq), 0)
    col = lax.broadcasted_iota(I32, (tq, tq), 1)
    past = row < col

    def tile(kb, carry, masked):
        start = pl.multiple_of(kb * tq, tq)
        k_all = k_ref[0, pl.ds(start, tq), :]
        v_all = v_ref[0, kb]
        zs = [_dot(k_all[:, h * SLAB:(h + 1) * SLAB], q_ref[0, h * SLAB:(h + 1) * SLAB, :])
              for h in range(heads)]
        lbs, exts = [], []
        for h in range(heads):
            z = zs[h]
            sp = jnp.maximum(z, 0.0) + jnp.log2(1.0 + jnp.exp2(-jnp.abs(z)))
            lbs.append(z - sp)
            if masked:
                sp = jnp.where(past, sp, 0.0)
            hi, lo = _split2(sp)
            exts.append([_dot(uu, jnp.concatenate([hi[ci * cb:(ci + 1) * cb, :],
                                                   lo[ci * cb:(ci + 1) * cb, :]], axis=0))
                         for ci in range(nchunk)])
        out = []
        for h in range(heads):
            run, acc = carry[h]
            chunks = [None] * nchunk
            for ci in reversed(range(nchunk)):
                rows = slice(ci * cb, (ci + 1) * cb)
                a = jnp.exp2(lbs[h][rows, :] - exts[h][ci][:cb, :] - run)
                if masked:
                    a = jnp.where(past[rows, :], a, 0.0)
                chunks[ci] = a.astype(BF16)
                run = run + exts[h][ci][cb:cb + 1, :]
            v = v_all[h * HEAD_DIM:(h + 1) * HEAD_DIM, :]
            acc = acc + _dot(v, jnp.concatenate(chunks, axis=0))
            out.append((run, acc))
        return tuple(out)

    init = tuple((jnp.zeros((1, tq), F32), jnp.zeros((HEAD_DIM, tq), F32)) for _ in range(heads))
    carry = tile(qi, init, True)
    final = lax.fori_loop(0, qi, lambda i, cr: tile(qi - 1 - i, cr, False), carry)
    for h in range(heads):
        _head_norm_store(final[h][1], g_ref, o_ref, h)


def _attention(kernel, qt, k, vt, g_col, *, tq, heads, name):
    b, d_slab, s = qt.shape
    nk = vt.shape[1]
    d_val = N_HEADS * HEAD_DIM
    return pl.pallas_call(
        functools.partial(kernel, tq=tq, heads=heads),
        out_shape=jax.ShapeDtypeStruct((b, d_val, s), BF16),
        grid=(b, N_HEADS // heads, s // tq),
        in_specs=[pl.BlockSpec((1, heads * SLAB, tq), lambda i, h, j: (i, h, j)),
                  pl.BlockSpec((1, s, heads * SLAB), lambda i, h, j: (i, 0, h)),
                  pl.BlockSpec((1, nk, heads * HEAD_DIM, tq), lambda i, h, j: (i, 0, h, 0)),
                  pl.BlockSpec((heads * HEAD_DIM, 1), lambda i, h, j: (h, 0))],
        out_specs=pl.BlockSpec((1, heads * HEAD_DIM, tq), lambda i, h, j: (i, h, j)),
        compiler_params=_cparams(("arbitrary", "arbitrary", "arbitrary")),
        name=name,
    )(qt, k, vt, g_col)


def _outproj_kernel(of_ref, os_ref, x_ref, gate_ref, sc_ref, sh_ref, wa_ref, wb_ref,
                    g_ref, b_ref, wr_ref, x1_ref, up_ref, lg_ref, *, alpha):
    d = x_ref.shape[2]
    mix = _dot_tn(of_ref[0], wa_ref[...]) + _dot_tn(os_ref[0], wb_ref[...])
    x1 = _layer_norm(alpha * x_ref[0] + (1.0 + gate_ref[0]) * mix, g_ref[...], b_ref[...])
    x1_ref[...] = x1
    u2 = x1 * (1.0 + sc_ref[0]) + sh_ref[0]
    up_ref[...] = _pack(u2)
    lg_ref[...] = _dot3_nt(wr_ref[...], u2)


def _outproj(of_t, os_t, x, gate, sc, sh, wa, wb, g, bb, wr_t, *, t, alpha):
    b, s, d = x.shape
    n = b * s
    nt = s // t
    e = wr_t.shape[0]
    d_val = of_t.shape[1]
    mod_spec = pl.BlockSpec((1, 1, d), lambda i, j: (i, 0, 0))
    return pl.pallas_call(
        functools.partial(_outproj_kernel, alpha=alpha),
        out_shape=(jax.ShapeDtypeStruct((n, d), F32),
                   jax.ShapeDtypeStruct((n, d // 2), U32),
                   jax.ShapeDtypeStruct((e, n), F32)),
        grid=(b, nt),
        in_specs=[pl.BlockSpec((1, d_val, t), lambda i, j: (i, 0, j)),
                  pl.BlockSpec((1, d_val, t), lambda i, j: (i, 0, j)),
                  pl.BlockSpec((1, t, d), lambda i, j: (i, j, 0)),
                  mod_spec, mod_spec, mod_spec,
                  pl.BlockSpec(wa.shape, lambda i, j: (0, 0)),
                  pl.BlockSpec(wb.shape, lambda i, j: (0, 0)),
                  pl.BlockSpec((1, d), lambda i, j: (0, 0)),
                  pl.BlockSpec((1, d), lambda i, j: (0, 0)),
                  pl.BlockSpec(wr_t.shape, lambda i, j: (0, 0))],
        out_specs=(pl.BlockSpec((t, d), lambda i, j: (i * nt + j, 0)),
                   pl.BlockSpec((t, d // 2), lambda i, j: (i * nt + j, 0)),
                   pl.BlockSpec((e, t), lambda i, j: (0, i * nt + j))),
        compiler_params=_cparams(("arbitrary", "arbitrary")),
        name="outproj",
    )(of_t, os_t, x, gate, sc, sh, wa, wb, g, bb, wr_t)


def _route_kernel(lg_ref, bias_ref, idx_ref, rank_ref, w_ref, cnt_ref, *, t):
    @pl.when(pl.program_id(0) == 0)
    def _():
        cnt_ref[...] = jnp.zeros_like(cnt_ref)

    scores = jax.nn.sigmoid(lg_ref[...])
    sel = scores + bias_ref[...]
    neg_inf = -jnp.inf

    gscore = []
    for gi in range(N_GROUPS):
        blk = sel[gi * GROUP_SIZE:(gi + 1) * GROUP_SIZE, :]
        m1 = jnp.max(blk, axis=0, keepdims=True)
        is_max = blk == m1
        n_max = jnp.sum(jnp.where(is_max, 1.0, 0.0), axis=0, keepdims=True)
        m2 = jnp.max(jnp.where(is_max, neg_inf, blk), axis=0, keepdims=True)
        gscore.append(m1 + jnp.where(n_max >= 2.0, m1, m2))

    parts = []
    for gi in range(N_GROUPS):
        beaten = jnp.zeros_like(gscore[gi])
        for gj in range(N_GROUPS):
            if gj == gi:
                continue
            wins = (gscore[gj] > gscore[gi]) if gj > gi else (gscore[gj] >= gscore[gi])
            beaten = beaten + jnp.where(wins, 1.0, 0.0)
        keep = beaten < float(TOP_GROUPS)
        blk = sel[gi * GROUP_SIZE:(gi + 1) * GROUP_SIZE, :]
        parts.append(jnp.where(keep, blk, neg_inf))
    cand = jnp.concatenate(parts, axis=0)

    eidx = lax.broadcasted_iota(I32, cand.shape, 0).astype(F32)
    chosen = jnp.zeros(cand.shape, F32)
    idx_rows, w_rows = [], []
    for _ in range(TOP_K):
        mx = jnp.max(cand, axis=0, keepdims=True)
        idx = jnp.min(jnp.where(cand == mx, eidx, float(N_EXPERTS)), axis=0, keepdims=True)
        hit = eidx == idx
        w_rows.append(jnp.sum(jnp.where(hit, scores, 0.0), axis=0, keepdims=True))
        idx_rows.append(idx)
        cand = jnp.where(hit, neg_inf, cand)
        chosen = jnp.where(hit, 1.0, chosen)

    w_sum = w_rows[0]
    for wk in w_rows[1:]:
        w_sum = w_sum + wk

    r = lax.broadcasted_iota(I32, (t, t), 0)
    c = lax.broadcasted_iota(I32, (t, t), 1)
    before = jnp.where(r < c, 1.0, 0.0).astype(BF16)
    prefix = _dot(chosen.astype(BF16), before) + cnt_ref[...]
    for k in range(TOP_K):
        hit = eidx == idx_rows[k]
        rank = jnp.sum(jnp.where(hit, prefix, 0.0), axis=0, keepdims=True)
        idx_ref[k:k + 1, :] = idx_rows[k].astype(I32)
        rank_ref[k:k + 1, :] = rank.astype(I32)
        w_ref[k:k + 1, :] = w_rows[k] / w_sum * ROUTED_SCALE
    cnt_ref[...] = cnt_ref[...] + jnp.sum(chosen, axis=1, keepdims=True)


def _route(logits_t, bias_col, *, t):
    e, n = logits_t.shape
    row_spec = pl.BlockSpec((TOP_K, t), lambda i: (0, i))
    return pl.pallas_call(
        functools.partial(_route_kernel, t=t),
        out_shape=(jax.ShapeDtypeStruct((TOP_K, n), I32),
                   jax.ShapeDtypeStruct((TOP_K, n), I32),
                   jax.ShapeDtypeStruct((TOP_K, n), F32),
                   jax.ShapeDtypeStruct((e, 1), F32)),
        grid=(n // t,),
        in_specs=[pl.BlockSpec((e, t), lambda i: (0, i)),
                  pl.BlockSpec((e, 1), lambda i: (0, 0))],
        out_specs=(row_spec, row_spec, row_spec, pl.BlockSpec((e, 1), lambda i: (0, 0))),
        compiler_params=_cparams(("arbitrary",)),
        name="route",
    )(logits_t, bias_col)


def _dest_kernel(cnt_ref, idx_ref, rank_ref, dest_ref, bs_ref, nb_ref, pstart_ref):
    e = cnt_ref.shape[0]

    @pl.when(pl.program_id(0) == 0)
    def _():
        nblk = jnp.floor((cnt_ref[...] + float(MOE_BLOCK - 1)) * (1.0 / MOE_BLOCK))
        hi = jnp.floor(nblk * (1.0 / 32.0))
        lo = nblk - 32.0 * hi
        r = lax.broadcasted_iota(I32, (e, e), 0)
        c = lax.broadcasted_iota(I32, (e, e), 1)
        below = jnp.where(c < r, 1.0, 0.0).astype(BF16)
        hi_b = jnp.broadcast_to(hi, (e, SLAB)).astype(BF16)
        lo_b = jnp.broadcast_to(lo, (e, SLAB)).astype(BF16)
        bstart = 32.0 * _dot(below, hi_b) + _dot(below, lo_b)
        pstart_ref[...] = bstart[:, 0:1] * float(MOE_BLOCK)
        bs_ref[...] = bstart[:, 0:1].astype(I32)
        nb_ref[...] = nblk.astype(I32)

    eidx = lax.broadcasted_iota(I32, (e, idx_ref.shape[1]), 0)
    for k in range(TOP_K):
        hit = eidx == idx_ref[k:k + 1, :]
        base = jnp.sum(jnp.where(hit, pstart_ref[...], 0.0), axis=0, keepdims=True)
        dest_ref[k:k + 1, :] = base.astype(I32) + rank_ref[k:k + 1, :]


def _dest(counts, idx_t, rank_t, *, t):
    e = counts.shape[0]
    n = idx_t.shape[1]
    row_spec = pl.BlockSpec((TOP_K, t), lambda i: (0, i))
    return pl.pallas_call(
        _dest_kernel,
        out_shape=(jax.ShapeDtypeStruct((TOP_K, n), I32),
                   jax.ShapeDtypeStruct((e, 1), I32),
                   jax.ShapeDtypeStruct((e, 1), I32)),
        grid=(n // t,),
        in_specs=[pl.BlockSpec((e, 1), lambda i: (0, 0)), row_spec, row_spec],
        out_specs=(row_spec,
                   pl.BlockSpec((e, 1), lambda i: (0, 0)),
                   pl.BlockSpec((e, 1), lambda i: (0, 0))),
        scratch_shapes=[pltpu.VMEM((e, 1), F32)],
        compiler_params=_cparams(("arbitrary",)),
        name="dest",
    )(counts, idx_t, rank_t)


def _row_copy(src_ref, src_row, dst_ref, dst_row, sem):
    return pltpu.make_async_copy(src_ref.at[pl.ds(src_row, 1)], dst_ref.at[pl.ds(dst_row, 1)], sem)


def _dispatch_kernel(dest_ref, up_ref, xs_in_ref, xs_ref, sem, *, t):
    del xs_in_ref

    def issue(j, _):
        for k in range(TOP_K):
            _row_copy(up_ref, j, xs_ref, dest_ref[k * t + j], sem).start(priority=k % 2)
        return 0

    lax.fori_loop(0, t, issue, 0)

    def drain(j, _):
        for k in range(TOP_K):
            _row_copy(up_ref, j, xs_ref, dest_ref[k * t + j], sem).wait()
        return 0

    lax.fori_loop(0, t, drain, 0)


def _dispatch(dest_flat, up, xs_zero, *, t):
    n = up.shape[0]
    return pl.pallas_call(
        functools.partial(_dispatch_kernel, t=t),
        out_shape=jax.ShapeDtypeStruct(xs_zero.shape, xs_zero.dtype),
        grid=(n // t,),
        in_specs=[pl.BlockSpec((TOP_K * t,), lambda i: (i,), memory_space=pltpu.SMEM),
                  pl.BlockSpec((t, up.shape[1]), lambda i: (i, 0)),
                  pl.BlockSpec(memory_space=pl.ANY)],
        out_specs=pl.BlockSpec(memory_space=pl.ANY),
        scratch_shapes=[pltpu.SemaphoreType.DMA(())],
        input_output_aliases={2: 0},
        compiler_params=_cparams(("arbitrary",)),
        name="dispatch",
    )(dest_flat, up, xs_zero)


def _unpack(words):
    lo, hi = _unpack_f32(words)
    return lo.astype(BF16), hi.astype(BF16)


def _expert_kernel(bs_ref, nb_ref, xs_ref, wg_ref, wu_ref, wd_ref, ys_ref,
                   wgu_s, wd_s, xbuf, ybuf, xsem, ysem, *, n_blocks):
    e = pl.program_id(0)
    d_half = xbuf.shape[2]
    d_exp = wg_ref.shape[2]
    nb = nb_ref[e]
    first = bs_ref[e]

    def rows_of(blk):
        return pl.ds(pl.multiple_of(blk * MOE_BLOCK, MOE_BLOCK), MOE_BLOCK)

    def x_copy(c, slot):
        return pltpu.make_async_copy(xs_ref.at[rows_of(first + c)], xbuf.at[slot], xsem.at[slot])

    def y_copy(blk, slot):
        return pltpu.make_async_copy(ybuf.at[slot], ys_ref.at[rows_of(blk)], ysem.at[slot])

    @pl.when(nb > 0)
    def _():
        x_copy(0, 0).start()
        wgu_s[:, :d_exp] = wg_ref[0].astype(BF16)
        wgu_s[:, d_exp:] = wu_ref[0].astype(BF16)
        wd_s[...] = wd_ref[0].astype(BF16)

        def block(c, _):
            slot = lax.rem(c, 2)
            x_copy(c, slot).wait()

            @pl.when(c + 1 < nb)
            def _():
                x_copy(c + 1, 1 - slot).start()

            @pl.when(c >= 2)
            def _():
                y_copy(first + c - 2, slot).wait()

            rows = MOE_BLOCK // MOE_SPLIT
            for part in range(MOE_SPLIT):
                sl = slice(part * rows, (part + 1) * rows)
                lo, hi = _unpack(xbuf[slot, sl, :])
                gu = _dot(lo, wgu_s[:d_half, :]) + _dot(hi, wgu_s[d_half:, :])
                gate = gu[:, :d_exp]
                h = gate * jax.nn.sigmoid(gate) * gu[:, d_exp:]
                ybuf[slot, sl, :] = _pack(_dot(h.astype(BF16), wd_s[...]))
            y_copy(first + c, slot).start()
            return 0

        lax.fori_loop(0, nb, block, 0)

        @pl.when(nb >= 2)
        def _():
            y_copy(first + nb - 2, lax.rem(nb, 2)).wait()

        y_copy(first + nb - 1, lax.rem(nb - 1, 2)).wait()

    @pl.when(e == pl.num_programs(0) - 1)
    def _():
        used = first + nb
        ybuf[0] = jnp.zeros(ybuf.shape[1:], ybuf.dtype)

        def fill(blk, _):
            y_copy(blk, 0).start()
            return 0

        def done(blk, _):
            y_copy(blk, 0).wait()
            return 0

        lax.fori_loop(used, n_blocks, fill, 0)
        lax.fori_loop(used, n_blocks, done, 0)


def _experts(blk_start, blk_count, xs, wg, wu, wd):
    cap, d_half = xs.shape
    d = 2 * d_half
    n_exp, _, d_exp = wg.shape
    grid_spec = pltpu.PrefetchScalarGridSpec(
        num_scalar_prefetch=2,
        grid=(n_exp,),
        in_specs=[pl.BlockSpec(memory_space=pl.ANY),
                  pl.BlockSpec((1, d, d_exp), lambda i, bs, nb: (i, 0, 0)),
                  pl.BlockSpec((1, d, d_exp), lambda i, bs, nb: (i, 0, 0)),
                  pl.BlockSpec((1, d_exp, d), lambda i, bs, nb: (i, 0, 0))],
        out_specs=pl.BlockSpec(memory_space=pl.ANY),
        scratch_shapes=[pltpu.VMEM((d, 2 * d_exp), BF16), pltpu.VMEM((d_exp, d), BF16),
                        pltpu.VMEM((2, MOE_BLOCK, d_half), U32), pltpu.VMEM((2, MOE_BLOCK, d_half), U32),
                        pltpu.SemaphoreType.DMA((2,)), pltpu.SemaphoreType.DMA((2,))],
    )
    return pl.pallas_call(
        functools.partial(_expert_kernel, n_blocks=cap // MOE_BLOCK),
        out_shape=jax.ShapeDtypeStruct((cap, d_half), U32),
        grid_spec=grid_spec,
        compiler_params=_cparams(("arbitrary",)),
        name="experts",
    )(blk_start, blk_count, xs, wg, wu, wd)


def _combine_kernel(dest_ref, ys_ref, w_ref, x1_ref, gate_ref, sc_ref, sh_ref,
                    wgu_ref, wd_ref, g_ref, b_ref, o_ref, buf, sem, *, t, alpha):
    def issue(j, _):
        for k in range(TOP_K):
            _row_copy(ys_ref, dest_ref[k * t + j], buf.at[k], j, sem).start(priority=k % 2)
        return 0

    lax.fori_loop(0, t, issue, 0)

    x1 = x1_ref[...]
    ub = (x1 * (1.0 + sc_ref[0]) + sh_ref[0]).astype(BF16)
    d_sh = wd_ref.shape[0]
    gu = _dot(ub, wgu_ref[...])
    gate = gu[:, :d_sh]
    h = gate * jax.nn.sigmoid(gate) * gu[:, d_sh:]
    ffn = _dot(h.astype(BF16), wd_ref[...])

    def drain(j, _):
        for k in range(TOP_K):
            _row_copy(ys_ref, dest_ref[k * t + j], buf.at[k], j, sem).wait()
        return 0

    lax.fori_loop(0, t, drain, 0)

    w = w_ref[...]
    d_half = buf.shape[2]
    lo_sum = ffn[:, :d_half]
    hi_sum = ffn[:, d_half:]
    for k in range(TOP_K):
        lo, hi = _unpack_f32(buf[k])
        lo_sum = lo_sum + w[:, k:k + 1] * lo
        hi_sum = hi_sum + w[:, k:k + 1] * hi
    ffn = jnp.concatenate([lo_sum, hi_sum], axis=1)
    o_ref[...] = _layer_norm(alpha * x1 + (1.0 + gate_ref[0]) * ffn, g_ref[...], b_ref[...])


def _combine(dest_flat, ys, w_tok, x1, gate, sc, sh, wgu, wd, g, bb, *, t, seq, alpha):
    n, d = x1.shape
    per_batch = seq // t
    mod_spec = pl.BlockSpec((1, 1, d), lambda i: (i // per_batch, 0, 0))
    return pl.pallas_call(
        functools.partial(_combine_kernel, t=t, alpha=alpha),
        out_shape=jax.ShapeDtypeStruct((n, d), F32),
        grid=(n // t,),
        in_specs=[pl.BlockSpec((TOP_K * t,), lambda i: (i,), memory_space=pltpu.SMEM),
                  pl.BlockSpec(memory_space=pl.ANY),
                  pl.BlockSpec((t, TOP_K), lambda i: (i, 0)),
                  pl.BlockSpec((t, d), lambda i: (i, 0)),
                  mod_spec, mod_spec, mod_spec,
                  pl.BlockSpec(wgu.shape, lambda i: (0, 0)),
                  pl.BlockSpec(wd.shape, lambda i: (0, 0)),
                  pl.BlockSpec((1, d), lambda i: (0, 0)),
                  pl.BlockSpec((1, d), lambda i: (0, 0))],
        out_specs=pl.BlockSpec((t, d), lambda i: (i, 0)),
        scratch_shapes=[pltpu.VMEM((TOP_K, t, d // 2), U32), pltpu.SemaphoreType.DMA(())],
        compiler_params=_cparams(("arbitrary",)),
        name="combine",
    )(dest_flat, ys, w_tok, x1, gate, sc, sh, wgu, wd, g, bb)


def _slab_cols(w, scale=1.0):
    d = w.shape[0]
    w = (w * scale).reshape(d, N_HEADS, HEAD_DIM)
    w = jnp.pad(w, ((0, 0), (0, 0), (0, SLAB - HEAD_DIM)))
    return w.reshape(d, N_HEADS * SLAB)


def _placement():
    h = jnp.arange(SLAB)[:, None]
    c = jnp.arange(N_HEADS * SLAB)[None, :]
    mats = [((h < N_HEADS) & (c == h * SLAB + HEAD_DIM + p)) for p in range(3)]
    return jnp.stack(mats).astype(BF16)


def _tile_major(rows, t):
    k, n = rows.shape
    return rows.reshape(k, n // t, t).transpose(1, 0, 2).reshape(-1)


def _layer(x, c_pad, w_ada, b_ada, w_in, b_f, fox_g, sb_g, w_out, ln1_g, ln1_b,
           w_router, router_bias, w_gate_e, w_up_e, w_down_e,
           w_gate_sh, w_up_sh, w_down_sh, ln2_g, ln2_b, *, alpha,
           t_proj=512, t_attn=512, attn_heads=4, t_route=512, t_moe=128):
    b, s, d = x.shape
    n = b * s
    d_val = N_HEADS * HEAD_DIM
    scale = HEAD_DIM ** -0.5 * LOG2_E

    ada = _ada(c_pad, w_ada, b_ada.reshape(1, -1))[:b]
    shift1, scale1, gate1, shift2, scale2, gate2 = [m[:, None, :] for m in jnp.split(ada, 6, axis=-1)]

    q_f, k_f, v_f, q_s, k_s, v_s, w_f = jnp.split(
        w_in, [d_val, 2 * d_val, 3 * d_val, 4 * d_val, 5 * d_val, 6 * d_val], axis=1)
    w_f = jnp.pad(w_f, ((0, 0), (0, SLAB - N_HEADS)))
    wtok = jnp.concatenate([_slab_cols(k_f), _slab_cols(k_s), w_f], axis=1).astype(BF16)
    wfeat = jnp.concatenate([_slab_cols(q_f, scale), _slab_cols(q_s, scale), v_f, v_s], axis=1).T.astype(BF16)
    bf_row = jnp.pad(b_f, (0, SLAB - N_HEADS)).reshape(1, SLAB)

    kf, ks, qf_t, qs_t, vf_t, vs_t = _inproj(x, scale1, shift1, wtok, wfeat, bf_row, _placement(),
                                             t=t_proj, tk=t_attn)
    of_t = _attention(_fox_kernel, qf_t, kf, vf_t, fox_g.reshape(-1, 1),
                      tq=t_attn, heads=attn_heads, name="fox")
    os_t = _attention(_sb_kernel, qs_t, ks, vs_t, sb_g.reshape(-1, 1),
                      tq=t_attn, heads=attn_heads, name="sb")

    x1, u_packed, logits_t = _outproj(
        of_t, os_t, x, gate1, scale2, shift2,
        w_out[:d_val].astype(BF16), w_out[d_val:].astype(BF16),
        ln1_g.reshape(1, d), ln1_b.reshape(1, d), w_router.T, t=t_proj, alpha=alpha)

    idx_t, rank_t, w_t, counts = _route(logits_t, router_bias.reshape(-1, 1), t=t_route)
    n_blocks = -(-(n * TOP_K) // MOE_BLOCK) + N_EXPERTS
    dest_t, blk_start, blk_count = _dest(counts, idx_t, rank_t, t=t_route)
    dest_flat = _tile_major(dest_t, t_moe)

    xs = _dispatch(dest_flat, u_packed, jnp.zeros((n_blocks * MOE_BLOCK, d // 2), U32), t=t_moe)
    ys = _experts(blk_start.reshape(-1), blk_count.reshape(-1), xs, w_gate_e, w_up_e, w_down_e)

    wgu_sh = jnp.concatenate([w_gate_sh, w_up_sh], axis=1).astype(BF16)
    out = _combine(dest_flat, ys, w_t.T, x1, gate2, scale2, shift2, wgu_sh, w_down_sh.astype(BF16),
                   ln2_g.reshape(1, d), ln2_b.reshape(1, d), t=t_moe, seq=s, alpha=alpha)
    return out.reshape(b, s, d)


def kernel(x, c, w_ada, b_ada, w_in, b_f, fox_norm_g, sb_norm_g, w_out, ln1_g, ln1_b, w_router, router_bias, w_gate_e, w_up_e, w_down_e, w_gate_sh, w_up_sh, w_down_sh, ln2_g, ln2_b):
    depth = w_ada.shape[0]
    alpha = (2.0 * depth) ** 0.25
    c_pad = jnp.pad(c, ((0, (-c.shape[0]) % 8), (0, 0)))
    for l in range(depth):
        x = _layer(x, c_pad, w_ada[l], b_ada[l], w_in[l], b_f[l], fox_norm_g[l], sb_norm_g[l],
                   w_out[l], ln1_g[l], ln1_b[l], w_router[l], router_bias[l],
                   w_gate_e[l], w_up_e[l], w_down_e[l], w_gate_sh[l], w_up_sh[l], w_down_sh[l],
                   ln2_g[l], ln2_b[l], alpha=alpha)
    return x
```

```python
import functools

import jax
import jax.numpy as jnp
from jax import lax
from jax.experimental import pallas as pl
from jax.experimental.pallas import tpu as pltpu

F32 = jnp.float32
BF16 = jnp.bfloat16
I32 = jnp.int32
U32 = jnp.uint32

HEAD_DIM = 64
N_HEADS = 8
SLAB = 128
N_EXPERTS = 256
TOP_K = 8
N_GROUPS = 8
TOP_GROUPS = 4
GROUP_SIZE = N_EXPERTS // N_GROUPS
ROUTED_SCALE = 2.5
MOE_BLOCK = 256
MOE_SPLIT = 2
SB_CHUNK = 128
LN_EPS = 1e-5
RMS_EPS = 1e-6
NEG_BIG = -1e30
LOG2_E = 1.4426950408889634

VMEM_LIMIT = 56 * 1024 * 1024


def _cparams(sem):
    return pltpu.CompilerParams(dimension_semantics=sem, vmem_limit_bytes=VMEM_LIMIT)


def _dot(a, b):
    return jnp.dot(a, b, preferred_element_type=F32)


def _dot_nt(a, b):
    return lax.dot_general(a, b, (((1,), (1,)), ((), ())), preferred_element_type=F32)


def _dot_tn(a, b):
    return lax.dot_general(a, b, (((0,), (0,)), ((), ())), preferred_element_type=F32)


def _split2(x):
    hi = x.astype(BF16)
    lo = (x - hi.astype(F32)).astype(BF16)
    return hi, lo


def _split3(x):
    p1 = x.astype(BF16)
    r1 = x - p1.astype(F32)
    p2 = r1.astype(BF16)
    p3 = (r1 - p2.astype(F32)).astype(BF16)
    return p1, p2, p3


def _dot3(a, b):
    a_hi, a_lo = _split2(a)
    b_hi, b_lo = _split2(b)
    return _dot(a_hi, b_hi) + _dot(a_lo, b_hi) + _dot(a_hi, b_lo)


def _dot3_nt(a, b):
    a_hi, a_lo = _split2(a)
    b_hi, b_lo = _split2(b)
    return _dot_nt(a_hi, b_hi) + _dot_nt(a_lo, b_hi) + _dot_nt(a_hi, b_lo)


def _pack(x):
    half = x.shape[1] // 2
    lo = lax.bitcast_convert_type(x[:, :half].astype(BF16).astype(F32), U32)
    hi = lax.bitcast_convert_type(x[:, half:].astype(BF16).astype(F32), U32)
    return (lo >> 16) | (hi & jnp.uint32(0xFFFF0000))


def _unpack_f32(words):
    lo = lax.bitcast_convert_type(words << 16, F32)
    hi = lax.bitcast_convert_type(words & jnp.uint32(0xFFFF0000), F32)
    return lo, hi


def _softplus(z):
    return jnp.maximum(z, 0.0) + jnp.log(1.0 + jnp.exp(-jnp.abs(z)))


def _layer_norm(v, g, b):
    mu = jnp.mean(v, axis=-1, keepdims=True)
    d = v - mu
    var = jnp.mean(d * d, axis=-1, keepdims=True)
    return d * lax.rsqrt(var + LN_EPS) * g + b


def _ada_kernel(c_ref, w_ref, b_ref, o_ref):
    c = c_ref[...]
    s = c * jax.nn.sigmoid(c)
    o_ref[...] = _dot3(s, w_ref[...]) + b_ref[...]


def _ada(c_pad, w, b):
    rows, d = c_pad.shape
    n = w.shape[1]
    tn = 1024
    return pl.pallas_call(
        _ada_kernel,
        out_shape=jax.ShapeDtypeStruct((rows, n), F32),
        grid=(n // tn,),
        in_specs=[pl.BlockSpec((rows, d), lambda j: (0, 0)),
                  pl.BlockSpec((d, tn), lambda j: (0, j)),
                  pl.BlockSpec((1, tn), lambda j: (0, j))],
        out_specs=pl.BlockSpec((rows, tn), lambda j: (0, j)),
        compiler_params=_cparams(("arbitrary",)),
        name="ada",
    )(c_pad, w, b)


def _inproj_kernel(x_ref, sc_ref, sh_ref, wtok_ref, wfeat_ref, bf_ref, place_ref,
                   kf_ref, ks_ref, qf_ref, qs_ref, vf_ref, vs_ref, carry_ref, *, t, tk):
    d_slab = N_HEADS * SLAB
    d_val = N_HEADS * HEAD_DIM

    @pl.when(pl.program_id(1) == 0)
    def _():
        carry_ref[...] = jnp.zeros_like(carry_ref)

    u = x_ref[0] * (1.0 + sc_ref[0]) + sh_ref[0]
    ub = u.astype(BF16)

    flog = _dot(ub, wtok_ref[:, 2 * d_slab:]) + bf_ref[...]
    logf = -_softplus(-flog)
    lane = lax.broadcasted_iota(I32, logf.shape, 1)
    logf = jnp.where(lane < N_HEADS, logf, 0.0)
    row = lax.broadcasted_iota(I32, (t, t), 0)
    col = lax.broadcasted_iota(I32, (t, t), 1)
    tri = jnp.where(row >= col, 1.0, 0.0).astype(BF16)
    lf_hi, lf_lo = _split2(logf)
    cum = _dot(tri, lf_hi) + _dot(tri, lf_lo) + carry_ref[...]
    carry_ref[...] = cum[t - 1:t, :]
    n1, n2, n3 = _split3(-LOG2_E * cum)
    extras = _dot(n1, place_ref[0]) + _dot(n2, place_ref[1]) + _dot(n3, place_ref[2])

    kf_ref[0] = (_dot(ub, wtok_ref[:, :d_slab]) + extras).astype(BF16)
    ks_ref[0] = _dot(ub, wtok_ref[:, d_slab:2 * d_slab]).astype(BF16)

    qf = _dot_nt(wfeat_ref[:d_slab, :], ub)
    r = lax.broadcasted_iota(I32, qf.shape, 0) % SLAB
    qf = jnp.where((r >= HEAD_DIM) & (r < HEAD_DIM + 3), 1.0, qf)
    qf_ref[0] = qf.astype(BF16)
    qs_ref[0] = _dot_nt(wfeat_ref[d_slab:2 * d_slab, :], ub).astype(BF16)
    vf = _dot_nt(wfeat_ref[2 * d_slab:2 * d_slab + d_val, :], ub).astype(BF16)
    vs = _dot_nt(wfeat_ref[2 * d_slab + d_val:, :], ub).astype(BF16)
    for c in range(t // tk):
        vf_ref[0, c] = vf[:, c * tk:(c + 1) * tk]
        vs_ref[0, c] = vs[:, c * tk:(c + 1) * tk]


def _inproj(x, sc, sh, wtok, wfeat, bf_row, place, *, t, tk):
    b, s, d = x.shape
    d_slab = N_HEADS * SLAB
    d_val = N_HEADS * HEAD_DIM
    nt = s // t
    out_shape = (
        jax.ShapeDtypeStruct((b, s, d_slab), BF16),
        jax.ShapeDtypeStruct((b, s, d_slab), BF16),
        jax.ShapeDtypeStruct((b, d_slab, s), BF16),
        jax.ShapeDtypeStruct((b, d_slab, s), BF16),
        jax.ShapeDtypeStruct((b, s // tk, d_val, tk), BF16),
        jax.ShapeDtypeStruct((b, s // tk, d_val, tk), BF16),
    )
    tok_spec = pl.BlockSpec((1, t, d_slab), lambda i, j: (i, j, 0))
    feat_spec = pl.BlockSpec((1, d_slab, t), lambda i, j: (i, 0, j))
    val_spec = pl.BlockSpec((1, t // tk, d_val, tk), lambda i, j: (i, j, 0, 0))
    return pl.pallas_call(
        functools.partial(_inproj_kernel, t=t, tk=tk),
        out_shape=out_shape,
        grid=(b, nt),
        in_specs=[pl.BlockSpec((1, t, d), lambda i, j: (i, j, 0)),
                  pl.BlockSpec((1, 1, d), lambda i, j: (i, 0, 0)),
                  pl.BlockSpec((1, 1, d), lambda i, j: (i, 0, 0)),
                  pl.BlockSpec(wtok.shape, lambda i, j: (0, 0)),
                  pl.BlockSpec(wfeat.shape, lambda i, j: (0, 0)),
                  pl.BlockSpec((1, SLAB), lambda i, j: (0, 0)),
                  pl.BlockSpec(place.shape, lambda i, j: (0, 0, 0))],
        out_specs=(tok_spec, tok_spec, feat_spec, feat_spec, val_spec, val_spec),
        scratch_shapes=[pltpu.VMEM((1, SLAB), F32)],
        compiler_params=_cparams(("arbitrary", "arbitrary")),
        name="inproj",
    )(x, sc, sh, wtok, wfeat, bf_row, place)


def _head_norm_store(acc, g_ref, o_ref, h):
    rows = slice(h * HEAD_DIM, (h + 1) * HEAD_DIM)
    ms = jnp.mean(acc * acc, axis=0, keepdims=True)
    o_ref[0, rows, :] = (acc * lax.rsqrt(ms + RMS_EPS) * g_ref[rows, :]).astype(BF16)


def _fox_kernel(q_ref, k_ref, v_ref, g_ref, o_ref, *, tq, heads):
    qi = pl.program_id(2)

    def tile(kb, carry, masked):
        start = pl.multiple_of(kb * tq, tq)
        k_all = k_ref[0, pl.ds(start, tq), :]
        v_all = v_ref[0, kb]
        scores = [_dot(k_all[:, h * SLAB:(h + 1) * SLAB], q_ref[0, h * SLAB:(h + 1) * SLAB, :])
                  for h in range(heads)]
        out = []
        for h in range(heads):
            m, l, acc = carry[h]
            s = scores[h]
            if masked:
                kid = lax.broadcasted_iota(I32, s.shape, 0)
                qid = lax.broadcasted_iota(I32, s.shape, 1)
                s = jnp.where(kid <= qid, s, NEG_BIG)
            m_new = jnp.maximum(m, jnp.max(s, axis=0, keepdims=True))
            p = jnp.exp2(s - m_new)
            alpha = jnp.exp2(m - m_new)
            l = alpha * l + jnp.sum(p, axis=0, keepdims=True)
            v = v_all[h * HEAD_DIM:(h + 1) * HEAD_DIM, :]
            acc = alpha * acc + _dot(v, p.astype(BF16))
            out.append((m_new, l, acc))
        return tuple(out)

    init = tuple((jnp.full((1, tq), NEG_BIG, F32), jnp.zeros((1, tq), F32),
                  jnp.zeros((HEAD_DIM, tq), F32)) for _ in range(heads))
    carry = lax.fori_loop(0, qi, lambda kb, c: tile(kb, c, False), init)
    final = tile(qi, carry, True)
    for h in range(heads):
        _, l, acc = final[h]
        _head_norm_store(acc / l, g_ref, o_ref, h)


def _sb_kernel(q_ref, k_ref, v_ref, g_ref, o_ref, *, tq, heads):
    qi = pl.program_id(2)
    cb = SB_CHUNK
    nchunk = tq // cb
    r = lax.broadcasted_iota(I32, (cb + 16, 2 * cb), 0)
    c = lax.broadcasted_iota(I32, (cb + 16, 2 * cb), 1) % cb
    uu = jnp.where(((r < cb) & (c > r)) | (r == cb), 1.0, 0.0).astype(BF16)
    row = lax.broadcasted_iota(I32, (tq, tq), 0)
    col = lax.broadcasted_iota(I32, (tq, tq), 1)
    past = row < col

    def tile(kb, carry, masked):
        start = pl.multiple_of(kb * tq, tq)
        k_all = k_ref[0, pl.ds(start, tq), :]
        v_all = v_ref[0, kb]
        zs = [_dot(k_all[:, h * SLAB:(h + 1) * SLAB], q_ref[0, h * SLAB:(h + 1) * SLAB, :])
              for h in range(heads)]
        lbs, exts = [], []
        for h in range(heads):
            z = zs[h]
            sp = jnp.maximum(z, 0.0) + jnp.log2(1.0 + jnp.exp2(-jnp.abs(z)))
            lbs.append(z - sp)
            if masked:
                sp = jnp.where(past, sp, 0.0)
            hi, lo = _split2(sp)
            exts.append([_dot(uu, jnp.concatenate([hi[ci * cb:(ci + 1) * cb, :],
                                                   lo[ci * cb:(ci + 1) * cb, :]], axis=0))
                         for ci in range(nchunk)])
        out = []
        for h in range(heads):
            run, acc = carry[h]
            chunks = [None] * nchunk
            for ci in reversed(range(nchunk)):
                rows = slice(ci * cb, (ci + 1) * cb)
                a = jnp.exp2(lbs[h][rows, :] - exts[h][ci][:cb, :] - run)
                if masked:
                    a = jnp.where(past[rows, :], a, 0.0)
                chunks[ci] = a.astype(BF16)
                run = run + exts[h][ci][cb:cb + 1, :]
            v = v_all[h * HEAD_DIM:(h + 1) * HEAD_DIM, :]
            acc = acc + _dot(v, jnp.concatenate(chunks, axis=0))
            out.append((run, acc))
        return tuple(out)

    init = tuple((jnp.zeros((1, tq), F32), jnp.zeros((HEAD_DIM, tq), F32)) for _ in range(heads))
    carry = tile(qi, init, True)
    final = lax.fori_loop(0, qi, lambda i, cr: tile(qi - 1 - i, cr, False), carry)
    for h in range(heads):
        _head_norm_store(final[h][1], g_ref, o_ref, h)


def _attention(kernel, qt, k, vt, g_col, *, tq, heads, name):
    b, d_slab, s = qt.shape
    nk = vt.shape[1]
    d_val = N_HEADS * HEAD_DIM
    return pl.pallas_call(
        functools.partial(kernel, tq=tq, heads=heads),
        out_shape=jax.ShapeDtypeStruct((b, d_val, s), BF16),
        grid=(b, N_HEADS // heads, s // tq),
        in_specs=[pl.BlockSpec((1, heads * SLAB, tq), lambda i, h, j: (i, h, j)),
                  pl.BlockSpec((1, s, heads * SLAB), lambda i, h, j: (i, 0, h)),
                  pl.BlockSpec((1, nk, heads * HEAD_DIM, tq), lambda i, h, j: (i, 0, h, 0)),
                  pl.BlockSpec((heads * HEAD_DIM, 1), lambda i, h, j: (h, 0))],
        out_specs=pl.BlockSpec((1, heads * HEAD_DIM, tq), lambda i, h, j: (i, h, j)),
        compiler_params=_cparams(("arbitrary", "arbitrary", "arbitrary")),
        name=name,
    )(qt, k, vt, g_col)


def _outproj_kernel(of_ref, os_ref, x_ref, gate_ref, sc_ref, sh_ref, wa_ref, wb_ref,
                    g_ref, b_ref, wr_ref, x1_ref, up_ref, lg_ref, *, alpha):
    d = x_ref.shape[2]
    mix = _dot_tn(of_ref[0], wa_ref[...]) + _dot_tn(os_ref[0], wb_ref[...])
    x1 = _layer_norm(alpha * x_ref[0] + (1.0 + gate_ref[0]) * mix, g_ref[...], b_ref[...])
    x1_ref[...] = x1
    u2 = x1 * (1.0 + sc_ref[0]) + sh_ref[0]
    up_ref[...] = _pack(u2)
    lg_ref[...] = _dot3_nt(wr_ref[...], u2)


def _outproj(of_t, os_t, x, gate, sc, sh, wa, wb, g, bb, wr_t, *, t, alpha):
    b, s, d = x.shape
    n = b * s
    nt = s // t
    e = wr_t.shape[0]
    d_val = of_t.shape[1]
    mod_spec = pl.BlockSpec((1, 1, d), lambda i, j: (i, 0, 0))
    return pl.pallas_call(
        functools.partial(_outproj_kernel, alpha=alpha),
        out_shape=(jax.ShapeDtypeStruct((n, d), F32),
                   jax.ShapeDtypeStruct((n, d // 2), U32),
                   jax.ShapeDtypeStruct((e, n), F32)),
        grid=(b, nt),
        in_specs=[pl.BlockSpec((1, d_val, t), lambda i, j: (i, 0, j)),
                  pl.BlockSpec((1, d_val, t), lambda i, j: (i, 0, j)),
                  pl.BlockSpec((1, t, d), lambda i, j: (i, j, 0)),
                  mod_spec, mod_spec, mod_spec,
                  pl.BlockSpec(wa.shape, lambda i, j: (0, 0)),
                  pl.BlockSpec(wb.shape, lambda i, j: (0, 0)),
                  pl.BlockSpec((1, d), lambda i, j: (0, 0)),
                  pl.BlockSpec((1, d), lambda i, j: (0, 0)),
                  pl.BlockSpec(wr_t.shape, lambda i, j: (0, 0))],
        out_specs=(pl.BlockSpec((t, d), lambda i, j: (i * nt + j, 0)),
                   pl.BlockSpec((t, d // 2), lambda i, j: (i * nt + j, 0)),
                   pl.BlockSpec((e, t), lambda i, j: (0, i * nt + j))),
        compiler_params=_cparams(("arbitrary", "arbitrary")),
        name="outproj",
    )(of_t, os_t, x, gate, sc, sh, wa, wb, g, bb, wr_t)


def _route_kernel(lg_ref, bias_ref, idx_ref, rank_ref, w_ref, cnt_ref, *, t):
    @pl.when(pl.program_id(0) == 0)
    def _():
        cnt_ref[...] = jnp.zeros_like(cnt_ref)

    scores = jax.nn.sigmoid(lg_ref[...])
    sel = scores + bias_ref[...]
    neg_inf = -jnp.inf

    gscore = []
    for gi in range(N_GROUPS):
        blk = sel[gi * GROUP_SIZE:(gi + 1) * GROUP_SIZE, :]
        m1 = jnp.max(blk, axis=0, keepdims=True)
        is_max = blk == m1
        n_max = jnp.sum(jnp.where(is_max, 1.0, 0.0), axis=0, keepdims=True)
        m2 = jnp.max(jnp.where(is_max, neg_inf, blk), axis=0, keepdims=True)
        gscore.append(m1 + jnp.where(n_max >= 2.0, m1, m2))

    parts = []
    for gi in range(N_GROUPS):
        beaten = jnp.zeros_like(gscore[gi])
        for gj in range(N_GROUPS):
            if gj == gi:
                continue
            wins = (gscore[gj] > gscore[gi]) if gj > gi else (gscore[gj] >= gscore[gi])
            beaten = beaten + jnp.where(wins, 1.0, 0.0)
        keep = beaten < float(TOP_GROUPS)
        blk = sel[gi * GROUP_SIZE:(gi + 1) * GROUP_SIZE, :]
        parts.append(jnp.where(keep, blk, neg_inf))
    cand = jnp.concatenate(parts, axis=0)

    eidx = lax.broadcasted_iota(I32, cand.shape, 0).astype(F32)
    chosen = jnp.zeros(cand.shape, F32)
    idx_rows, w_rows = [], []
    for _ in range(TOP_K):
        mx = jnp.max(cand, axis=0, keepdims=True)
        idx = jnp.min(jnp.where(cand == mx, eidx, float(N_EXPERTS)), axis=0, keepdims=True)
        hit = eidx == idx
        w_rows.append(jnp.sum(jnp.where(hit, scores, 0.0), axis=0, keepdims=True))
        idx_rows.append(idx)
        cand = jnp.where(hit, neg_inf, cand)
        chosen = jnp.where(hit, 1.0, chosen)

    w_sum = w_rows[0]
    for wk in w_rows[1:]:
        w_sum = w_sum + wk

    r = lax.broadcasted_iota(I32, (t, t), 0)
    c = lax.broadcasted_iota(I32, (t, t), 1)
    before = jnp.where(r < c, 1.0, 0.0).astype(BF16)
    prefix = _dot(chosen.astype(BF16), before) + cnt_ref[...]
    for k in range(TOP_K):
        hit = eidx == idx_rows[k]
        rank = jnp.sum(jnp.where(hit, prefix, 0.0), axis=0, keepdims=True)
        idx_ref[k:k + 1, :] = idx_rows[k].astype(I32)
        rank_ref[k:k + 1, :] = rank.astype(I32)
        w_ref[k:k + 1, :] = w_rows[k] / w_sum * ROUTED_SCALE
    cnt_ref[...] = cnt_ref[...] + jnp.sum(chosen, axis=1, keepdims=True)


def _route(logits_t, bias_col, *, t):
    e, n = logits_t.shape
    row_spec = pl.BlockSpec((TOP_K, t), lambda i: (0, i))
    return pl.pallas_call(
        functools.partial(_route_kernel, t=t),
        out_shape=(jax.ShapeDtypeStruct((TOP_K, n), I32),
                   jax.ShapeDtypeStruct((TOP_K, n), I32),
                   jax.ShapeDtypeStruct((TOP_K, n), F32),
                   jax.ShapeDtypeStruct((e, 1), F32)),
        grid=(n // t,),
        in_specs=[pl.BlockSpec((e, t), lambda i: (0, i)),
                  pl.BlockSpec((e, 1), lambda i: (0, 0))],
        out_specs=(row_spec, row_spec, row_spec, pl.BlockSpec((e, 1), lambda i: (0, 0))),
        compiler_params=_cparams(("arbitrary",)),
        name="route",
    )(logits_t, bias_col)


def _dest_kernel(cnt_ref, idx_ref, rank_ref, dest_ref, bs_ref, nb_ref, pstart_ref):
    e = cnt_ref.shape[0]

    @pl.when(pl.program_id(0) == 0)
    def _():
        nblk = jnp.floor((cnt_ref[...] + float(MOE_BLOCK - 1)) * (1.0 / MOE_BLOCK))
        hi = jnp.floor(nblk * (1.0 / 32.0))
        lo = nblk - 32.0 * hi
        r = lax.broadcasted_iota(I32, (e, e), 0)
        c = lax.broadcasted_iota(I32, (e, e), 1)
        below = jnp.where(c < r, 1.0, 0.0).astype(BF16)
        hi_b = jnp.broadcast_to(hi, (e, SLAB)).astype(BF16)
        lo_b = jnp.broadcast_to(lo, (e, SLAB)).astype(BF16)
        bstart = 32.0 * _dot(below, hi_b) + _dot(below, lo_b)
        pstart_ref[...] = bstart[:, 0:1] * float(MOE_BLOCK)
        bs_ref[...] = bstart[:, 0:1].astype(I32)
        nb_ref[...] = nblk.astype(I32)

    eidx = lax.broadcasted_iota(I32, (e, idx_ref.shape[1]), 0)
    for k in range(TOP_K):
        hit = eidx == idx_ref[k:k + 1, :]
        base = jnp.sum(jnp.where(hit, pstart_ref[...], 0.0), axis=0, keepdims=True)
        dest_ref[k:k + 1, :] = base.astype(I32) + rank_ref[k:k + 1, :]


def _dest(counts, idx_t, rank_t, *, t):
    e = counts.shape[0]
    n = idx_t.shape[1]
    row_spec = pl.BlockSpec((TOP_K, t), lambda i: (0, i))
    return pl.pallas_call(
        _dest_kernel,
        out_shape=(jax.ShapeDtypeStruct((TOP_K, n), I32),
                   jax.ShapeDtypeStruct((e, 1), I32),
                   jax.ShapeDtypeStruct((e, 1), I32)),
        grid=(n // t,),
        in_specs=[pl.BlockSpec((e, 1), lambda i: (0, 0)), row_spec, row_spec],
        out_specs=(row_spec,
                   pl.BlockSpec((e, 1), lambda i: (0, 0)),
                   pl.BlockSpec((e, 1), lambda i: (0, 0))),
        scratch_shapes=[pltpu.VMEM((e, 1), F32)],
        compiler_params=_cparams(("arbitrary",)),
        name="dest",
    )(counts, idx_t, rank_t)


def _row_copy(src_ref, src_row, dst_ref, dst_row, sem):
    return pltpu.make_async_copy(src_ref.at[pl.ds(src_row, 1)], dst_ref.at[pl.ds(dst_row, 1)], sem)


def _dispatch_kernel(dest_ref, up_ref, xs_in_ref, xs_ref, sem, *, t):
    del xs_in_ref

    def issue(j, _):
        for k in range(TOP_K):
            _row_copy(up_ref, j, xs_ref, dest_ref[k * t + j], sem).start()
        return 0

    lax.fori_loop(0, t, issue, 0)

    def drain(j, _):
        for k in range(TOP_K):
            _row_copy(up_ref, j, xs_ref, dest_ref[k * t + j], sem).wait()
        return 0

    lax.fori_loop(0, t, drain, 0)


def _dispatch(dest_flat, up, xs_zero, *, t):
    n = up.shape[0]
    return pl.pallas_call(
        functools.partial(_dispatch_kernel, t=t),
        out_shape=jax.ShapeDtypeStruct(xs_zero.shape, xs_zero.dtype),
        grid=(n // t,),
        in_specs=[pl.BlockSpec((TOP_K * t,), lambda i: (i,), memory_space=pltpu.SMEM),
                  pl.BlockSpec((t, up.shape[1]), lambda i: (i, 0)),
                  pl.BlockSpec(memory_space=pl.ANY)],
        out_specs=pl.BlockSpec(memory_space=pl.ANY),
        scratch_shapes=[pltpu.SemaphoreType.DMA(())],
        input_output_aliases={2: 0},
        compiler_params=_cparams(("arbitrary",)),
        name="dispatch",
    )(dest_flat, up, xs_zero)


def _unpack(words):
    lo, hi = _unpack_f32(words)
    return lo.astype(BF16), hi.astype(BF16)


def _expert_kernel(bs_ref, nb_ref, xs_ref, wg_ref, wu_ref, wd_ref, ys_ref,
                   wgu_s, wd_s, xbuf, ybuf, xsem, ysem, *, n_blocks, n_experts):
    e = pl.program_id(0)
    last = n_experts - 1
    d_half = xbuf.shape[2]
    d_exp = wg_ref.shape[2]
    nb = nb_ref[e]
    first = bs_ref[e]
    used = bs_ref[last] + nb_ref[last]

    def rows_of(blk):
        return pl.ds(pl.multiple_of(blk * MOE_BLOCK, MOE_BLOCK), MOE_BLOCK)

    def x_copy(blk, slot):
        return pltpu.make_async_copy(xs_ref.at[rows_of(blk)], xbuf.at[slot], xsem.at[slot])

    def y_copy(blk, slot):
        return pltpu.make_async_copy(ybuf.at[slot], ys_ref.at[rows_of(blk)], ysem.at[slot])

    @pl.when((e == 0) & (used > 0))
    def _():
        x_copy(0, 0).start()

    @pl.when(nb > 0)
    def _():
        wgu_s[:, :d_exp] = wg_ref[0].astype(BF16)
        wgu_s[:, d_exp:] = wu_ref[0].astype(BF16)
        wd_s[...] = wd_ref[0].astype(BF16)

        def block(g, _):
            slot = lax.rem(g, 2)
            x_copy(g, slot).wait()

            @pl.when(g + 1 < used)
            def _():
                x_copy(g + 1, 1 - slot).start()

            @pl.when(g >= 2)
            def _():
                y_copy(g - 2, slot).wait()

            rows = MOE_BLOCK // MOE_SPLIT
            for part in range(MOE_SPLIT):
                sl = slice(part * rows, (part + 1) * rows)
                lo, hi = _unpack(xbuf[slot, sl, :])
                gu = _dot(lo, wgu_s[:d_half, :]) + _dot(hi, wgu_s[d_half:, :])
                gate = gu[:, :d_exp]
                h = gate * jax.nn.sigmoid(gate) * gu[:, d_exp:]
                ybuf[slot, sl, :] = _pack(_dot(h.astype(BF16), wd_s[...]))
            y_copy(g, slot).start()
            return 0

        lax.fori_loop(first, first + nb, block, 0)

    @pl.when(e == last)
    def _():
        @pl.when(used >= 2)
        def _():
            y_copy(used - 2, lax.rem(used, 2)).wait()

        @pl.when(used >= 1)
        def _():
            y_copy(used - 1, lax.rem(used - 1, 2)).wait()

        ybuf[0] = jnp.zeros(ybuf.shape[1:], ybuf.dtype)

        def fill(blk, _):
            y_copy(blk, 0).start()
            return 0

        def done(blk, _):
            y_copy(blk, 0).wait()
            return 0

        lax.fori_loop(used, n_blocks, fill, 0)
        lax.fori_loop(used, n_blocks, done, 0)


def _experts(blk_start, blk_count, xs, wg, wu, wd):
    cap, d_half = xs.shape
    d = 2 * d_half
    n_exp, _, d_exp = wg.shape
    grid_spec = pltpu.PrefetchScalarGridSpec(
        num_scalar_prefetch=2,
        grid=(n_exp,),
        in_specs=[pl.BlockSpec(memory_space=pl.ANY),
                  pl.BlockSpec((1, d, d_exp), lambda i, bs, nb: (i, 0, 0)),
                  pl.BlockSpec((1, d, d_exp), lambda i, bs, nb: (i, 0, 0)),
                  pl.BlockSpec((1, d_exp, d), lambda i, bs, nb: (i, 0, 0))],
        out_specs=pl.BlockSpec(memory_space=pl.ANY),
        scratch_shapes=[pltpu.VMEM((d, 2 * d_exp), BF16), pltpu.VMEM((d_exp, d), BF16),
                        pltpu.VMEM((2, MOE_BLOCK, d_half), U32), pltpu.VMEM((2, MOE_BLOCK, d_half), U32),
                        pltpu.SemaphoreType.DMA((2,)), pltpu.SemaphoreType.DMA((2,))],
    )
    return pl.pallas_call(
        functools.partial(_expert_kernel, n_blocks=cap // MOE_BLOCK, n_experts=n_exp),
        out_shape=jax.ShapeDtypeStruct((cap, d_half), U32),
        grid_spec=grid_spec,
        compiler_params=_cparams(("arbitrary",)),
        name="experts",
    )(blk_start, blk_count, xs, wg, wu, wd)


def _combine_kernel(dest_ref, ys_ref, w_ref, x1_ref, gate_ref, sc_ref, sh_ref,
                    wgu_ref, wd_ref, g_ref, b_ref, o_ref, buf, sem, *, t, alpha):
    def issue(j, _):
        for k in range(TOP_K):
            _row_copy(ys_ref, dest_ref[k * t + j], buf.at[k], j, sem).start()
        return 0

    lax.fori_loop(0, t, issue, 0)

    x1 = x1_ref[...]
    ub = (x1 * (1.0 + sc_ref[0]) + sh_ref[0]).astype(BF16)
    d_sh = wd_ref.shape[0]
    gu = _dot(ub, wgu_ref[...])
    gate = gu[:, :d_sh]
    h = gate * jax.nn.sigmoid(gate) * gu[:, d_sh:]
    ffn = _dot(h.astype(BF16), wd_ref[...])

    def drain(j, _):
        for k in range(TOP_K):
            _row_copy(ys_ref, dest_ref[k * t + j], buf.at[k], j, sem).wait()
        return 0

    lax.fori_loop(0, t, drain, 0)

    w = w_ref[...]
    d_half = buf.shape[2]
    lo_sum = ffn[:, :d_half]
    hi_sum = ffn[:, d_half:]
    for k in range(TOP_K):
        lo, hi = _unpack_f32(buf[k])
        lo_sum = lo_sum + w[:, k:k + 1] * lo
        hi_sum = hi_sum + w[:, k:k + 1] * hi
    ffn = jnp.concatenate([lo_sum, hi_sum], axis=1)
    o_ref[...] = _layer_norm(alpha * x1 + (1.0 + gate_ref[0]) * ffn, g_ref[...], b_ref[...])


def _combine(dest_flat, ys, w_tok, x1, gate, sc, sh, wgu, wd, g, bb, *, t, seq, alpha):
    n, d = x1.shape
    per_batch = seq // t
    mod_spec = pl.BlockSpec((1, 1, d), lambda i: (i // per_batch, 0, 0))
    return pl.pallas_call(
        functools.partial(_combine_kernel, t=t, alpha=alpha),
        out_shape=jax.ShapeDtypeStruct((n, d), F32),
        grid=(n // t,),
        in_specs=[pl.BlockSpec((TOP_K * t,), lambda i: (i,), memory_space=pltpu.SMEM),
                  pl.BlockSpec(memory_space=pl.ANY),
                  pl.BlockSpec((t, TOP_K), lambda i: (i, 0)),
                  pl.BlockSpec((t, d), lambda i: (i, 0)),
                  mod_spec, mod_spec, mod_spec,
                  pl.BlockSpec(wgu.shape, lambda i: (0, 0)),
                  pl.BlockSpec(wd.shape, lambda i: (0, 0)),
                  pl.BlockSpec((1, d), lambda i: (0, 0)),
                  pl.BlockSpec((1, d), lambda i: (0, 0))],
        out_specs=pl.BlockSpec((t, d), lambda i: (i, 0)),
        scratch_shapes=[pltpu.VMEM((TOP_K, t, d // 2), U32), pltpu.SemaphoreType.DMA(())],
        compiler_params=_cparams(("arbitrary",)),
        name="combine",
    )(dest_flat, ys, w_tok, x1, gate, sc, sh, wgu, wd, g, bb)


def _slab_cols(w, scale=1.0):
    d = w.shape[0]
    w = (w * scale).reshape(d, N_HEADS, HEAD_DIM)
    w = jnp.pad(w, ((0, 0), (0, 0), (0, SLAB - HEAD_DIM)))
    return w.reshape(d, N_HEADS * SLAB)


def _placement():
    h = jnp.arange(SLAB)[:, None]
    c = jnp.arange(N_HEADS * SLAB)[None, :]
    mats = [((h < N_HEADS) & (c == h * SLAB + HEAD_DIM + p)) for p in range(3)]
    return jnp.stack(mats).astype(BF16)


def _tile_major(rows, t):
    k, n = rows.shape
    return rows.reshape(k, n // t, t).transpose(1, 0, 2).reshape(-1)


def _layer(x, c_pad, w_ada, b_ada, w_in, b_f, fox_g, sb_g, w_out, ln1_g, ln1_b,
           w_router, router_bias, w_gate_e, w_up_e, w_down_e,
           w_gate_sh, w_up_sh, w_down_sh, ln2_g, ln2_b, *, alpha,
           t_proj=512, t_attn=512, attn_heads=4, t_route=512, t_moe=128):
    b, s, d = x.shape
    n = b * s
    d_val = N_HEADS * HEAD_DIM
    scale = HEAD_DIM ** -0.5 * LOG2_E

    ada = _ada(c_pad, w_ada, b_ada.reshape(1, -1))[:b]
    shift1, scale1, gate1, shift2, scale2, gate2 = [m[:, None, :] for m in jnp.split(ada, 6, axis=-1)]

    q_f, k_f, v_f, q_s, k_s, v_s, w_f = jnp.split(
        w_in, [d_val, 2 * d_val, 3 * d_val, 4 * d_val, 5 * d_val, 6 * d_val], axis=1)
    w_f = jnp.pad(w_f, ((0, 0), (0, SLAB - N_HEADS)))
    wtok = jnp.concatenate([_slab_cols(k_f), _slab_cols(k_s), w_f], axis=1).astype(BF16)
    wfeat = jnp.concatenate([_slab_cols(q_f, scale), _slab_cols(q_s, scale), v_f, v_s], axis=1).T.astype(BF16)
    bf_row = jnp.pad(b_f, (0, SLAB - N_HEADS)).reshape(1, SLAB)

    kf, ks, qf_t, qs_t, vf_t, vs_t = _inproj(x, scale1, shift1, wtok, wfeat, bf_row, _placement(),
                                             t=t_proj, tk=t_attn)
    of_t = _attention(_fox_kernel, qf_t, kf, vf_t, fox_g.reshape(-1, 1),
                      tq=t_attn, heads=attn_heads, name="fox")
    os_t = _attention(_sb_kernel, qs_t, ks, vs_t, sb_g.reshape(-1, 1),
                      tq=t_attn, heads=attn_heads, name="sb")

    x1, u_packed, logits_t = _outproj(
        of_t, os_t, x, gate1, scale2, shift2,
        w_out[:d_val].astype(BF16), w_out[d_val:].astype(BF16),
        ln1_g.reshape(1, d), ln1_b.reshape(1, d), w_router.T, t=t_proj, alpha=alpha)

    idx_t, rank_t, w_t, counts = _route(logits_t, router_bias.reshape(-1, 1), t=t_route)
    n_blocks = -(-(n * TOP_K) // MOE_BLOCK) + N_EXPERTS
    dest_t, blk_start, blk_count = _dest(counts, idx_t, rank_t, t=t_route)
    dest_flat = _tile_major(dest_t, t_moe)

    xs = _dispatch(dest_flat, u_packed, jnp.zeros((n_blocks * MOE_BLOCK, d // 2), U32), t=t_moe)
    ys = _experts(blk_start.reshape(-1), blk_count.reshape(-1), xs, w_gate_e, w_up_e, w_down_e)

    wgu_sh = jnp.concatenate([w_gate_sh, w_up_sh], axis=1).astype(BF16)
    out = _combine(dest_flat, ys, w_t.T, x1, gate2, scale2, shift2, wgu_sh, w_down_sh.astype(BF16),
                   ln2_g.reshape(1, d), ln2_b.reshape(1, d), t=t_moe, seq=s, alpha=alpha)
    return out.reshape(b, s, d)


def kernel(x, c, w_ada, b_ada, w_in, b_f, fox_norm_g, sb_norm_g, w_out, ln1_g, ln1_b, w_router, router_bias, w_gate_e, w_up_e, w_down_e, w_gate_sh, w_up_sh, w_down_sh, ln2_g, ln2_b):
    depth = w_ada.shape[0]
    alpha = (2.0 * depth) ** 0.25
    c_pad = jnp.pad(c, ((0, (-c.shape[0]) % 8), (0, 0)))
    for l in range(depth):
        x = _layer(x, c_pad, w_ada[l], b_ada[l], w_in[l], b_f[l], fox_norm_g[l], sb_norm_g[l],
                   w_out[l], ln1_g[l], ln1_b[l], w_router[l], router_bias[l],
                   w_gate_e[l], w_up_e[l], w_down_e[l], w_gate_sh[l], w_up_sh[l], w_down_sh[l],
                   ln2_g[l], ln2_b[l], alpha=alpha)
    return x
```

```python
import functools

import jax
import jax.numpy as jnp
from jax import lax
from jax.experimental import pallas as pl
from jax.experimental.pallas import tpu as pltpu
from jax.experimental.pallas import tpu_sc as plsc

F32 = jnp.float32
BF16 = jnp.bfloat16
I32 = jnp.int32
U32 = jnp.uint32

HEAD_DIM = 64
N_HEADS = 8
SLAB = 128
N_EXPERTS = 256
TOP_K = 8
N_GROUPS = 8
TOP_GROUPS = 4
GROUP_SIZE = N_EXPERTS // N_GROUPS
ROUTED_SCALE = 2.5
MOE_BLOCK = 256
MOE_SPLIT = 2
SB_CHUNK = 128
LN_EPS = 1e-5
RMS_EPS = 1e-6
NEG_BIG = -1e30
LOG2_E = 1.4426950408889634

VMEM_LIMIT = 56 * 1024 * 1024


def _cparams(sem):
    return pltpu.CompilerParams(dimension_semantics=sem, vmem_limit_bytes=VMEM_LIMIT)


def _dot(a, b):
    return jnp.dot(a, b, preferred_element_type=F32)


def _dot_nt(a, b):
    return lax.dot_general(a, b, (((1,), (1,)), ((), ())), preferred_element_type=F32)


def _dot_tn(a, b):
    return lax.dot_general(a, b, (((0,), (0,)), ((), ())), preferred_element_type=F32)


def _split2(x):
    hi = x.astype(BF16)
    lo = (x - hi.astype(F32)).astype(BF16)
    return hi, lo


def _split3(x):
    p1 = x.astype(BF16)
    r1 = x - p1.astype(F32)
    p2 = r1.astype(BF16)
    p3 = (r1 - p2.astype(F32)).astype(BF16)
    return p1, p2, p3


def _dot3(a, b):
    a_hi, a_lo = _split2(a)
    b_hi, b_lo = _split2(b)
    return _dot(a_hi, b_hi) + _dot(a_lo, b_hi) + _dot(a_hi, b_lo)


def _dot3_nt(a, b):
    a_hi, a_lo = _split2(a)
    b_hi, b_lo = _split2(b)
    return _dot_nt(a_hi, b_hi) + _dot_nt(a_lo, b_hi) + _dot_nt(a_hi, b_lo)


def _pack(x):
    half = x.shape[1] // 2
    lo = lax.bitcast_convert_type(x[:, :half].astype(BF16).astype(F32), U32)
    hi = lax.bitcast_convert_type(x[:, half:].astype(BF16).astype(F32), U32)
    return (lo >> 16) | (hi & jnp.uint32(0xFFFF0000))


def _unpack_f32(words):
    lo = lax.bitcast_convert_type(words << 16, F32)
    hi = lax.bitcast_convert_type(words & jnp.uint32(0xFFFF0000), F32)
    return lo, hi


def _softplus(z):
    return jnp.maximum(z, 0.0) + jnp.log(1.0 + jnp.exp(-jnp.abs(z)))


def _layer_norm(v, g, b):
    mu = jnp.mean(v, axis=-1, keepdims=True)
    d = v - mu
    var = jnp.mean(d * d, axis=-1, keepdims=True)
    return d * lax.rsqrt(var + LN_EPS) * g + b


def _ada_kernel(c_ref, w_ref, b_ref, o_ref):
    c = c_ref[...]
    s = c * jax.nn.sigmoid(c)
    o_ref[...] = _dot3(s, w_ref[...]) + b_ref[...]


def _ada(c_pad, w, b):
    rows, d = c_pad.shape
    n = w.shape[1]
    tn = 1024
    return pl.pallas_call(
        _ada_kernel,
        out_shape=jax.ShapeDtypeStruct((rows, n), F32),
        grid=(n // tn,),
        in_specs=[pl.BlockSpec((rows, d), lambda j: (0, 0)),
                  pl.BlockSpec((d, tn), lambda j: (0, j)),
                  pl.BlockSpec((1, tn), lambda j: (0, j))],
        out_specs=pl.BlockSpec((rows, tn), lambda j: (0, j)),
        compiler_params=_cparams(("arbitrary",)),
        name="ada",
    )(c_pad, w, b)


def _inproj_kernel(x_ref, sc_ref, sh_ref, wtok_ref, wfeat_ref, bf_ref, place_ref,
                   kf_ref, ks_ref, qf_ref, qs_ref, vf_ref, vs_ref, carry_ref, *, t, tk):
    d_slab = N_HEADS * SLAB
    d_val = N_HEADS * HEAD_DIM

    @pl.when(pl.program_id(1) == 0)
    def _():
        carry_ref[...] = jnp.zeros_like(carry_ref)

    u = x_ref[0] * (1.0 + sc_ref[0]) + sh_ref[0]
    ub = u.astype(BF16)

    flog = _dot(ub, wtok_ref[:, 2 * d_slab:]) + bf_ref[...]
    logf = -_softplus(-flog)
    lane = lax.broadcasted_iota(I32, logf.shape, 1)
    logf = jnp.where(lane < N_HEADS, logf, 0.0)
    row = lax.broadcasted_iota(I32, (t, t), 0)
    col = lax.broadcasted_iota(I32, (t, t), 1)
    tri = jnp.where(row >= col, 1.0, 0.0).astype(BF16)
    lf_hi, lf_lo = _split2(logf)
    cum = _dot(tri, lf_hi) + _dot(tri, lf_lo) + carry_ref[...]
    carry_ref[...] = cum[t - 1:t, :]
    n1, n2, n3 = _split3(-LOG2_E * cum)
    extras = _dot(n1, place_ref[0]) + _dot(n2, place_ref[1]) + _dot(n3, place_ref[2])

    kf_ref[0] = (_dot(ub, wtok_ref[:, :d_slab]) + extras).astype(BF16)
    ks_ref[0] = _dot(ub, wtok_ref[:, d_slab:2 * d_slab]).astype(BF16)

    qf = _dot_nt(wfeat_ref[:d_slab, :], ub)
    r = lax.broadcasted_iota(I32, qf.shape, 0) % SLAB
    qf = jnp.where((r >= HEAD_DIM) & (r < HEAD_DIM + 3), 1.0, qf)
    qf_ref[0] = qf.astype(BF16)
    qs_ref[0] = _dot_nt(wfeat_ref[d_slab:2 * d_slab, :], ub).astype(BF16)
    vf = _dot_nt(wfeat_ref[2 * d_slab:2 * d_slab + d_val, :], ub).astype(BF16)
    vs = _dot_nt(wfeat_ref[2 * d_slab + d_val:, :], ub).astype(BF16)
    for c in range(t // tk):
        vf_ref[0, c] = vf[:, c * tk:(c + 1) * tk]
        vs_ref[0, c] = vs[:, c * tk:(c + 1) * tk]


def _inproj(x, sc, sh, wtok, wfeat, bf_row, place, *, t, tk):
    b, s, d = x.shape
    d_slab = N_HEADS * SLAB
    d_val = N_HEADS * HEAD_DIM
    nt = s // t
    out_shape = (
        jax.ShapeDtypeStruct((b, s, d_slab), BF16),
        jax.ShapeDtypeStruct((b, s, d_slab), BF16),
        jax.ShapeDtypeStruct((b, d_slab, s), BF16),
        jax.ShapeDtypeStruct((b, d_slab, s), BF16),
        jax.ShapeDtypeStruct((b, s // tk, d_val, tk), BF16),
        jax.ShapeDtypeStruct((b, s // tk, d_val, tk), BF16),
    )
    tok_spec = pl.BlockSpec((1, t, d_slab), lambda i, j: (i, j, 0))
    feat_spec = pl.BlockSpec((1, d_slab, t), lambda i, j: (i, 0, j))
    val_spec = pl.BlockSpec((1, t // tk, d_val, tk), lambda i, j: (i, j, 0, 0))
    return pl.pallas_call(
        functools.partial(_inproj_kernel, t=t, tk=tk),
        out_shape=out_shape,
        grid=(b, nt),
        in_specs=[pl.BlockSpec((1, t, d), lambda i, j: (i, j, 0)),
                  pl.BlockSpec((1, 1, d), lambda i, j: (i, 0, 0)),
                  pl.BlockSpec((1, 1, d), lambda i, j: (i, 0, 0)),
                  pl.BlockSpec(wtok.shape, lambda i, j: (0, 0)),
                  pl.BlockSpec(wfeat.shape, lambda i, j: (0, 0)),
                  pl.BlockSpec((1, SLAB), lambda i, j: (0, 0)),
                  pl.BlockSpec(place.shape, lambda i, j: (0, 0, 0))],
        out_specs=(tok_spec, tok_spec, feat_spec, feat_spec, val_spec, val_spec),
        scratch_shapes=[pltpu.VMEM((1, SLAB), F32)],
        compiler_params=_cparams(("arbitrary", "arbitrary")),
        name="inproj",
    )(x, sc, sh, wtok, wfeat, bf_row, place)


def _head_norm_store(acc, g_ref, o_ref, h):
    rows = slice(h * HEAD_DIM, (h + 1) * HEAD_DIM)
    ms = jnp.mean(acc * acc, axis=0, keepdims=True)
    o_ref[0, rows, :] = (acc * lax.rsqrt(ms + RMS_EPS) * g_ref[rows, :]).astype(BF16)


def _fox_kernel(q_ref, k_ref, v_ref, g_ref, o_ref, *, tq, heads):
    qi = pl.program_id(2)

    def tile(kb, carry, masked):
        start = pl.multiple_of(kb * tq, tq)
        k_all = k_ref[0, pl.ds(start, tq), :]
        v_all = v_ref[0, kb]
        scores = [_dot(k_all[:, h * SLAB:(h + 1) * SLAB], q_ref[0, h * SLAB:(h + 1) * SLAB, :])
                  for h in range(heads)]
        out = []
        for h in range(heads):
            m, l, acc = carry[h]
            s = scores[h]
            if masked:
                kid = lax.broadcasted_iota(I32, s.shape, 0)
                qid = lax.broadcasted_iota(I32, s.shape, 1)
                s = jnp.where(kid <= qid, s, NEG_BIG)
            m_new = jnp.maximum(m, jnp.max(s, axis=0, keepdims=True))
            p = jnp.exp2(s - m_new)
            alpha = jnp.exp2(m - m_new)
            l = alpha * l + jnp.sum(p, axis=0, keepdims=True)
            v = v_all[h * HEAD_DIM:(h + 1) * HEAD_DIM, :]
            acc = alpha * acc + _dot(v, p.astype(BF16))
            out.append((m_new, l, acc))
        return tuple(out)

    init = tuple((jnp.full((1, tq), NEG_BIG, F32), jnp.zeros((1, tq), F32),
                  jnp.zeros((HEAD_DIM, tq), F32)) for _ in range(heads))
    carry = lax.fori_loop(0, qi, lambda kb, c: tile(kb, c, False), init)
    final = tile(qi, carry, True)
    for h in range(heads):
        _, l, acc = final[h]
        _head_norm_store(acc / l, g_ref, o_ref, h)


def _sb_kernel(q_ref, k_ref, v_ref, g_ref, o_ref, *, tq, heads):
    qi = pl.program_id(2)
    cb = SB_CHUNK
    nchunk = tq // cb
    r = lax.broadcasted_iota(I32, (cb + 16, 2 * cb), 0)
    c = lax.broadcasted_iota(I32, (cb + 16, 2 * cb), 1) % cb
    uu = jnp.where(((r < cb) & (c > r)) | (r == cb), 1.0, 0.0).astype(BF16)
    row = lax.broadcasted_iota(I32, (tq, tq), 0)
    col = lax.broadcasted_iota(I32, (tq, tq), 1)
    past = row < col

    def tile(kb, carry, masked):
        start = pl.multiple_of(kb * tq, tq)
        k_all = k_ref[0, pl.ds(start, tq), :]
        v_all = v_ref[0, kb]
        zs = [_dot(k_all[:, h * SLAB:(h + 1) * SLAB], q_ref[0, h * SLAB:(h + 1) * SLAB, :])
              for h in range(heads)]
        lbs, exts = [], []
        for h in range(heads):
            z = zs[h]
            sp = jnp.maximum(z, 0.0) + jnp.log2(1.0 + jnp.exp2(-jnp.abs(z)))
            lbs.append(z - sp)
            if masked:
                sp = jnp.where(past, sp, 0.0)
            hi, lo = _split2(sp)
            exts.append([_dot(uu, jnp.concatenate([hi[ci * cb:(ci + 1) * cb, :],
                                                   lo[ci * cb:(ci + 1) * cb, :]], axis=0))
                         for ci in range(nchunk)])
        out = []
        for h in range(heads):
            run, acc = carry[h]
            chunks = [None] * nchunk
            for ci in reversed(range(nchunk)):
                rows = slice(ci * cb, (ci + 1) * cb)
                a = jnp.exp2(lbs[h][rows, :] - exts[h][ci][:cb, :] - run)
                if masked:
                    a = jnp.where(past[rows, :], a, 0.0)
                chunks[ci] = a.astype(BF16)
                run = run + exts[h][ci][cb:cb + 1, :]
            v = v_all[h * HEAD_DIM:(h + 1) * HEAD_DIM, :]
            acc = acc + _dot(v, jnp.concatenate(chunks, axis=0))
            out.append((run, acc))
        return tuple(out)

    init = tuple((jnp.zeros((1, tq), F32), jnp.zeros((HEAD_DIM, tq), F32)) for _ in range(heads))
    carry = tile(qi, init, True)
    final = lax.fori_loop(0, qi, lambda i, cr: tile(qi - 1 - i, cr, False), carry)
    for h in range(heads):
        _head_norm_store(final[h][1], g_ref, o_ref, h)


def _attention(kernel, qt, k, vt, g_col, *, tq, heads, name):
    b, d_slab, s = qt.shape
    nk = vt.shape[1]
    d_val = N_HEADS * HEAD_DIM
    return pl.pallas_call(
        functools.partial(kernel, tq=tq, heads=heads),
        out_shape=jax.ShapeDtypeStruct((b, d_val, s), BF16),
        grid=(b, N_HEADS // heads, s // tq),
        in_specs=[pl.BlockSpec((1, heads * SLAB, tq), lambda i, h, j: (i, h, j)),
                  pl.BlockSpec((1, s, heads * SLAB), lambda i, h, j: (i, 0, h)),
                  pl.BlockSpec((1, nk, heads * HEAD_DIM, tq), lambda i, h, j: (i, 0, h, 0)),
                  pl.BlockSpec((heads * HEAD_DIM, 1), lambda i, h, j: (h, 0))],
        out_specs=pl.BlockSpec((1, heads * HEAD_DIM, tq), lambda i, h, j: (i, h, j)),
        compiler_params=_cparams(("arbitrary", "arbitrary", "arbitrary")),
        name=name,
    )(qt, k, vt, g_col)


def _outproj_kernel(of_ref, os_ref, x_ref, gate_ref, sc_ref, sh_ref, wa_ref, wb_ref,
                    g_ref, b_ref, wr_ref, x1_ref, up_ref, lg_ref, *, alpha):
    d = x_ref.shape[2]
    mix = _dot_tn(of_ref[0], wa_ref[...]) + _dot_tn(os_ref[0], wb_ref[...])
    x1 = _layer_norm(alpha * x_ref[0] + (1.0 + gate_ref[0]) * mix, g_ref[...], b_ref[...])
    x1_ref[...] = x1
    u2 = x1 * (1.0 + sc_ref[0]) + sh_ref[0]
    up_ref[...] = _pack(u2)
    lg_ref[...] = _dot3_nt(wr_ref[...], u2)


def _outproj(of_t, os_t, x, gate, sc, sh, wa, wb, g, bb, wr_t, *, t, alpha):
    b, s, d = x.shape
    n = b * s
    nt = s // t
    e = wr_t.shape[0]
    d_val = of_t.shape[1]
    mod_spec = pl.BlockSpec((1, 1, d), lambda i, j: (i, 0, 0))
    return pl.pallas_call(
        functools.partial(_outproj_kernel, alpha=alpha),
        out_shape=(jax.ShapeDtypeStruct((n, d), F32),
                   jax.ShapeDtypeStruct((n, d // 2), U32),
                   jax.ShapeDtypeStruct((e, n), F32)),
        grid=(b, nt),
        in_specs=[pl.BlockSpec((1, d_val, t), lambda i, j: (i, 0, j)),
                  pl.BlockSpec((1, d_val, t), lambda i, j: (i, 0, j)),
                  pl.BlockSpec((1, t, d), lambda i, j: (i, j, 0)),
                  mod_spec, mod_spec, mod_spec,
                  pl.BlockSpec(wa.shape, lambda i, j: (0, 0)),
                  pl.BlockSpec(wb.shape, lambda i, j: (0, 0)),
                  pl.BlockSpec((1, d), lambda i, j: (0, 0)),
                  pl.BlockSpec((1, d), lambda i, j: (0, 0)),
                  pl.BlockSpec(wr_t.shape, lambda i, j: (0, 0))],
        out_specs=(pl.BlockSpec((t, d), lambda i, j: (i * nt + j, 0)),
                   pl.BlockSpec((t, d // 2), lambda i, j: (i * nt + j, 0)),
                   pl.BlockSpec((e, t), lambda i, j: (0, i * nt + j))),
        compiler_params=_cparams(("arbitrary", "arbitrary")),
        name="outproj",
    )(of_t, os_t, x, gate, sc, sh, wa, wb, g, bb, wr_t)


def _route_kernel(lg_ref, bias_ref, idx_ref, rank_ref, w_ref, cnt_ref, *, t):
    @pl.when(pl.program_id(0) == 0)
    def _():
        cnt_ref[...] = jnp.zeros_like(cnt_ref)

    scores = jax.nn.sigmoid(lg_ref[...])
    sel = scores + bias_ref[...]
    neg_inf = -jnp.inf

    gscore = []
    for gi in range(N_GROUPS):
        blk = sel[gi * GROUP_SIZE:(gi + 1) * GROUP_SIZE, :]
        m1 = jnp.max(blk, axis=0, keepdims=True)
        is_max = blk == m1
        n_max = jnp.sum(jnp.where(is_max, 1.0, 0.0), axis=0, keepdims=True)
        m2 = jnp.max(jnp.where(is_max, neg_inf, blk), axis=0, keepdims=True)
        gscore.append(m1 + jnp.where(n_max >= 2.0, m1, m2))

    parts = []
    for gi in range(N_GROUPS):
        beaten = jnp.zeros_like(gscore[gi])
        for gj in range(N_GROUPS):
            if gj == gi:
                continue
            wins = (gscore[gj] > gscore[gi]) if gj > gi else (gscore[gj] >= gscore[gi])
            beaten = beaten + jnp.where(wins, 1.0, 0.0)
        keep = beaten < float(TOP_GROUPS)
        blk = sel[gi * GROUP_SIZE:(gi + 1) * GROUP_SIZE, :]
        parts.append(jnp.where(keep, blk, neg_inf))
    cand = jnp.concatenate(parts, axis=0)

    eidx = lax.broadcasted_iota(I32, cand.shape, 0).astype(F32)
    chosen = jnp.zeros(cand.shape, F32)
    idx_rows, w_rows = [], []
    for _ in range(TOP_K):
        mx = jnp.max(cand, axis=0, keepdims=True)
        idx = jnp.min(jnp.where(cand == mx, eidx, float(N_EXPERTS)), axis=0, keepdims=True)
        hit = eidx == idx
        w_rows.append(jnp.sum(jnp.where(hit, scores, 0.0), axis=0, keepdims=True))
        idx_rows.append(idx)
        cand = jnp.where(hit, neg_inf, cand)
        chosen = jnp.where(hit, 1.0, chosen)

    w_sum = w_rows[0]
    for wk in w_rows[1:]:
        w_sum = w_sum + wk

    r = lax.broadcasted_iota(I32, (t, t), 0)
    c = lax.broadcasted_iota(I32, (t, t), 1)
    before = jnp.where(r < c, 1.0, 0.0).astype(BF16)
    prefix = _dot(chosen.astype(BF16), before) + cnt_ref[...]
    for k in range(TOP_K):
        hit = eidx == idx_rows[k]
        rank = jnp.sum(jnp.where(hit, prefix, 0.0), axis=0, keepdims=True)
        idx_ref[k:k + 1, :] = idx_rows[k].astype(I32)
        rank_ref[k:k + 1, :] = rank.astype(I32)
        w_ref[k:k + 1, :] = w_rows[k] / w_sum * ROUTED_SCALE
    cnt_ref[...] = cnt_ref[...] + jnp.sum(chosen, axis=1, keepdims=True)


def _route(logits_t, bias_col, *, t):
    e, n = logits_t.shape
    row_spec = pl.BlockSpec((TOP_K, t), lambda i: (0, i))
    return pl.pallas_call(
        functools.partial(_route_kernel, t=t),
        out_shape=(jax.ShapeDtypeStruct((TOP_K, n), I32),
                   jax.ShapeDtypeStruct((TOP_K, n), I32),
                   jax.ShapeDtypeStruct((TOP_K, n), F32),
                   jax.ShapeDtypeStruct((e, 1), F32)),
        grid=(n // t,),
        in_specs=[pl.BlockSpec((e, t), lambda i: (0, i)),
                  pl.BlockSpec((e, 1), lambda i: (0, 0))],
        out_specs=(row_spec, row_spec, row_spec, pl.BlockSpec((e, 1), lambda i: (0, 0))),
        compiler_params=_cparams(("arbitrary",)),
        name="route",
    )(logits_t, bias_col)


def _dest_kernel(cnt_ref, idx_ref, rank_ref, dest_ref, bs_ref, nb_ref, pstart_ref):
    e = cnt_ref.shape[0]

    @pl.when(pl.program_id(0) == 0)
    def _():
        nblk = jnp.floor((cnt_ref[...] + float(MOE_BLOCK - 1)) * (1.0 / MOE_BLOCK))
        hi = jnp.floor(nblk * (1.0 / 32.0))
        lo = nblk - 32.0 * hi
        r = lax.broadcasted_iota(I32, (e, e), 0)
        c = lax.broadcasted_iota(I32, (e, e), 1)
        below = jnp.where(c < r, 1.0, 0.0).astype(BF16)
        hi_b = jnp.broadcast_to(hi, (e, SLAB)).astype(BF16)
        lo_b = jnp.broadcast_to(lo, (e, SLAB)).astype(BF16)
        bstart = 32.0 * _dot(below, hi_b) + _dot(below, lo_b)
        pstart_ref[...] = bstart[:, 0:1] * float(MOE_BLOCK)
        bs_ref[...] = bstart[:, 0:1].astype(I32)
        nb_ref[...] = nblk.astype(I32)

    eidx = lax.broadcasted_iota(I32, (e, idx_ref.shape[1]), 0)
    for k in range(TOP_K):
        hit = eidx == idx_ref[k:k + 1, :]
        base = jnp.sum(jnp.where(hit, pstart_ref[...], 0.0), axis=0, keepdims=True)
        dest_ref[k:k + 1, :] = base.astype(I32) + rank_ref[k:k + 1, :]


def _dest(counts, idx_t, rank_t, *, t):
    e = counts.shape[0]
    n = idx_t.shape[1]
    row_spec = pl.BlockSpec((TOP_K, t), lambda i: (0, i))
    return pl.pallas_call(
        _dest_kernel,
        out_shape=(jax.ShapeDtypeStruct((TOP_K, n), I32),
                   jax.ShapeDtypeStruct((e, 1), I32),
                   jax.ShapeDtypeStruct((e, 1), I32)),
        grid=(n // t,),
        in_specs=[pl.BlockSpec((e, 1), lambda i: (0, 0)), row_spec, row_spec],
        out_specs=(row_spec,
                   pl.BlockSpec((e, 1), lambda i: (0, 0)),
                   pl.BlockSpec((e, 1), lambda i: (0, 0))),
        scratch_shapes=[pltpu.VMEM((e, 1), F32)],
        compiler_params=_cparams(("arbitrary",)),
        name="dest",
    )(counts, idx_t, rank_t)


def _row_copy(src_ref, src_row, dst_ref, dst_row, sem):
    return pltpu.make_async_copy(src_ref.at[pl.ds(src_row, 1)], dst_ref.at[pl.ds(dst_row, 1)], sem)


def _dispatch_kernel(dest_ref, up_ref, xs_in_ref, xs_ref, sem, *, t):
    del xs_in_ref

    def issue(j, _):
        for k in range(TOP_K):
            _row_copy(up_ref, j, xs_ref, dest_ref[k * t + j], sem).start()
        return 0

    lax.fori_loop(0, t, issue, 0)

    def drain(j, _):
        for k in range(TOP_K):
            _row_copy(up_ref, j, xs_ref, dest_ref[k * t + j], sem).wait()
        return 0

    lax.fori_loop(0, t, drain, 0)


def _dispatch(dest_flat, up, xs_zero, *, t):
    n = up.shape[0]
    return pl.pallas_call(
        functools.partial(_dispatch_kernel, t=t),
        out_shape=jax.ShapeDtypeStruct(xs_zero.shape, xs_zero.dtype),
        grid=(n // t,),
        in_specs=[pl.BlockSpec((TOP_K * t,), lambda i: (i,), memory_space=pltpu.SMEM),
                  pl.BlockSpec((t, up.shape[1]), lambda i: (i, 0)),
                  pl.BlockSpec(memory_space=pl.ANY)],
        out_specs=pl.BlockSpec(memory_space=pl.ANY),
        scratch_shapes=[pltpu.SemaphoreType.DMA(())],
        input_output_aliases={2: 0},
        compiler_params=_cparams(("arbitrary",)),
        name="dispatch",
    )(dest_flat, up, xs_zero)


def _unpack(words):
    lo, hi = _unpack_f32(words)
    return lo.astype(BF16), hi.astype(BF16)


def _expert_kernel(bs_ref, nb_ref, xs_ref, wg_ref, wu_ref, wd_ref, ys_ref,
                   wgu_s, wd_s, xbuf, ybuf, xsem, ysem, *, n_blocks, n_experts):
    e = pl.program_id(0)
    last = n_experts - 1
    d_half = xbuf.shape[2]
    d_exp = wg_ref.shape[2]
    nb = nb_ref[e]
    first = bs_ref[e]
    used = bs_ref[last] + nb_ref[last]

    def rows_of(blk):
        return pl.ds(pl.multiple_of(blk * MOE_BLOCK, MOE_BLOCK), MOE_BLOCK)

    def x_copy(blk, slot):
        return pltpu.make_async_copy(xs_ref.at[rows_of(blk)], xbuf.at[slot], xsem.at[slot])

    def y_copy(blk, slot):
        return pltpu.make_async_copy(ybuf.at[slot], ys_ref.at[rows_of(blk)], ysem.at[slot])

    @pl.when((e == 0) & (used > 0))
    def _():
        x_copy(0, 0).start()

    @pl.when(nb > 0)
    def _():
        wgu_s[:, :d_exp] = wg_ref[0].astype(BF16)
        wgu_s[:, d_exp:] = wu_ref[0].astype(BF16)
        wd_s[...] = wd_ref[0].astype(BF16)

        def block(g, _):
            slot = lax.rem(g, 2)
            x_copy(g, slot).wait()

            @pl.when(g + 1 < used)
            def _():
                x_copy(g + 1, 1 - slot).start()

            @pl.when(g >= 2)
            def _():
                y_copy(g - 2, slot).wait()

            rows = MOE_BLOCK // MOE_SPLIT
            for part in range(MOE_SPLIT):
                sl = slice(part * rows, (part + 1) * rows)
                lo, hi = _unpack(xbuf[slot, sl, :])
                gu = _dot(lo, wgu_s[:d_half, :]) + _dot(hi, wgu_s[d_half:, :])
                gate = gu[:, :d_exp]
                h = gate * jax.nn.sigmoid(gate) * gu[:, d_exp:]
                ybuf[slot, sl, :] = _pack(_dot(h.astype(BF16), wd_s[...]))
            y_copy(g, slot).start()
            return 0

        lax.fori_loop(first, first + nb, block, 0)

    @pl.when(e == last)
    def _():
        @pl.when(used >= 2)
        def _():
            y_copy(used - 2, lax.rem(used, 2)).wait()

        @pl.when(used >= 1)
        def _():
            y_copy(used - 1, lax.rem(used - 1, 2)).wait()

        ybuf[0] = jnp.zeros(ybuf.shape[1:], ybuf.dtype)

        def fill(blk, _):
            y_copy(blk, 0).start()
            return 0

        def done(blk, _):
            y_copy(blk, 0).wait()
            return 0

        lax.fori_loop(used, n_blocks, fill, 0)
        lax.fori_loop(used, n_blocks, done, 0)


def _experts(blk_start, blk_count, xs, wg, wu, wd):
    cap, d_half = xs.shape
    d = 2 * d_half
    n_exp, _, d_exp = wg.shape
    grid_spec = pltpu.PrefetchScalarGridSpec(
        num_scalar_prefetch=2,
        grid=(n_exp,),
        in_specs=[pl.BlockSpec(memory_space=pl.ANY),
                  pl.BlockSpec((1, d, d_exp), lambda i, bs, nb: (i, 0, 0)),
                  pl.BlockSpec((1, d, d_exp), lambda i, bs, nb: (i, 0, 0)),
                  pl.BlockSpec((1, d_exp, d), lambda i, bs, nb: (i, 0, 0))],
        out_specs=pl.BlockSpec(memory_space=pl.ANY),
        scratch_shapes=[pltpu.VMEM((d, 2 * d_exp), BF16), pltpu.VMEM((d_exp, d), BF16),
                        pltpu.VMEM((2, MOE_BLOCK, d_half), U32), pltpu.VMEM((2, MOE_BLOCK, d_half), U32),
                        pltpu.SemaphoreType.DMA((2,)), pltpu.SemaphoreType.DMA((2,))],
    )
    return pl.pallas_call(
        functools.partial(_expert_kernel, n_blocks=cap // MOE_BLOCK, n_experts=n_exp),
        out_shape=jax.ShapeDtypeStruct((cap, d_half), U32),
        grid_spec=grid_spec,
        compiler_params=_cparams(("arbitrary",)),
        name="experts",
    )(blk_start, blk_count, xs, wg, wu, wd)


SC_CORES = 2
SC_SUBCORES = 16
SC_CHUNK = 64


def _sc_gather_kernel(table_ref, idx_ref, out_ref, idx0, idx1, rows0, rows1, sem0, sem1, *, per_worker):
    wid = lax.axis_index("s") * SC_CORES + lax.axis_index("c")
    base = wid * per_worker

    def body(i, _):
        off0 = pl.multiple_of(base + (2 * i) * SC_CHUNK, SC_CHUNK)
        off1 = pl.multiple_of(off0 + SC_CHUNK, SC_CHUNK)
        pltpu.sync_copy(idx_ref.at[pl.ds(off0, SC_CHUNK)], idx0)
        g0 = pltpu.async_copy(table_ref.at[idx0], rows0, sem0)
        pltpu.sync_copy(idx_ref.at[pl.ds(off1, SC_CHUNK)], idx1)
        g1 = pltpu.async_copy(table_ref.at[idx1], rows1, sem1)
        g0.wait()
        pltpu.sync_copy(rows0, out_ref.at[pl.ds(off0, SC_CHUNK)])
        g1.wait()
        pltpu.sync_copy(rows1, out_ref.at[pl.ds(off1, SC_CHUNK)])
        return 0

    lax.fori_loop(0, per_worker // (2 * SC_CHUNK), body, 0)


def _sc_gather(table, idx):
    rows = idx.shape[0]
    width = table.shape[1]
    workers = SC_CORES * SC_SUBCORES
    per_worker = rows // workers
    assert per_worker * workers == rows and per_worker % (2 * SC_CHUNK) == 0
    mesh = plsc.VectorSubcoreMesh(core_axis_name="c", subcore_axis_name="s")
    return pl.kernel(
        functools.partial(_sc_gather_kernel, per_worker=per_worker),
        out_type=jax.ShapeDtypeStruct((rows, width), table.dtype),
        mesh=mesh,
        scratch_types=[pltpu.VMEM((SC_CHUNK,), I32), pltpu.VMEM((SC_CHUNK,), I32),
                       pltpu.VMEM((SC_CHUNK, width), table.dtype), pltpu.VMEM((SC_CHUNK, width), table.dtype),
                       pltpu.SemaphoreType.DMA, pltpu.SemaphoreType.DMA],
        name="sc_gather",
    )(table, idx)


def _combine_kernel(yg_ref, w_ref, x1_ref, gate_ref, sc_ref, sh_ref,
                    wgu_ref, wd_ref, g_ref, b_ref, o_ref, *, alpha):
    x1 = x1_ref[...]
    ub = (x1 * (1.0 + sc_ref[0]) + sh_ref[0]).astype(BF16)
    d_sh = wd_ref.shape[0]
    gu = _dot(ub, wgu_ref[...])
    gate = gu[:, :d_sh]
    h = gate * jax.nn.sigmoid(gate) * gu[:, d_sh:]
    ffn = _dot(h.astype(BF16), wd_ref[...])

    w = w_ref[...]
    d_half = yg_ref.shape[3]
    lo_sum = ffn[:, :d_half]
    hi_sum = ffn[:, d_half:]
    for k in range(TOP_K):
        lo, hi = _unpack_f32(yg_ref[0, k])
        lo_sum = lo_sum + w[:, k:k + 1] * lo
        hi_sum = hi_sum + w[:, k:k + 1] * hi
    ffn = jnp.concatenate([lo_sum, hi_sum], axis=1)
    o_ref[...] = _layer_norm(alpha * x1 + (1.0 + gate_ref[0]) * ffn, g_ref[...], b_ref[...])


def _combine(yg, w_tok, x1, gate, sc, sh, wgu, wd, g, bb, *, t, seq, alpha):
    n, d = x1.shape
    per_batch = seq // t
    mod_spec = pl.BlockSpec((1, 1, d), lambda i: (i // per_batch, 0, 0))
    return pl.pallas_call(
        functools.partial(_combine_kernel, alpha=alpha),
        out_shape=jax.ShapeDtypeStruct((n, d), F32),
        grid=(n // t,),
        in_specs=[pl.BlockSpec((1, TOP_K, t, d // 2), lambda i: (i, 0, 0, 0)),
                  pl.BlockSpec((t, TOP_K), lambda i: (i, 0)),
                  pl.BlockSpec((t, d), lambda i: (i, 0)),
                  mod_spec, mod_spec, mod_spec,
                  pl.BlockSpec(wgu.shape, lambda i: (0, 0)),
                  pl.BlockSpec(wd.shape, lambda i: (0, 0)),
                  pl.BlockSpec((1, d), lambda i: (0, 0)),
                  pl.BlockSpec((1, d), lambda i: (0, 0))],
        out_specs=pl.BlockSpec((t, d), lambda i: (i, 0)),
        compiler_params=_cparams(("arbitrary",)),
        name="combine",
    )(yg, w_tok, x1, gate, sc, sh, wgu, wd, g, bb)


def _slab_cols(w, scale=1.0):
    d = w.shape[0]
    w = (w * scale).reshape(d, N_HEADS, HEAD_DIM)
    w = jnp.pad(w, ((0, 0), (0, 0), (0, SLAB - HEAD_DIM)))
    return w.reshape(d, N_HEADS * SLAB)


def _placement():
    h = jnp.arange(SLAB)[:, None]
    c = jnp.arange(N_HEADS * SLAB)[None, :]
    mats = [((h < N_HEADS) & (c == h * SLAB + HEAD_DIM + p)) for p in range(3)]
    return jnp.stack(mats).astype(BF16)


def _tile_major(rows, t):
    k, n = rows.shape
    return rows.reshape(k, n // t, t).transpose(1, 0, 2).reshape(-1)


def _layer(x, c_pad, w_ada, b_ada, w_in, b_f, fox_g, sb_g, w_out, ln1_g, ln1_b,
           w_router, router_bias, w_gate_e, w_up_e, w_down_e,
           w_gate_sh, w_up_sh, w_down_sh, ln2_g, ln2_b, *, alpha,
           t_proj=512, t_attn=512, attn_heads=4, t_route=512, t_moe=128):
    b, s, d = x.shape
    n = b * s
    d_val = N_HEADS * HEAD_DIM
    scale = HEAD_DIM ** -0.5 * LOG2_E

    ada = _ada(c_pad, w_ada, b_ada.reshape(1, -1))[:b]
    shift1, scale1, gate1, shift2, scale2, gate2 = [m[:, None, :] for m in jnp.split(ada, 6, axis=-1)]

    q_f, k_f, v_f, q_s, k_s, v_s, w_f = jnp.split(
        w_in, [d_val, 2 * d_val, 3 * d_val, 4 * d_val, 5 * d_val, 6 * d_val], axis=1)
    w_f = jnp.pad(w_f, ((0, 0), (0, SLAB - N_HEADS)))
    wtok = jnp.concatenate([_slab_cols(k_f), _slab_cols(k_s), w_f], axis=1).astype(BF16)
    wfeat = jnp.concatenate([_slab_cols(q_f, scale), _slab_cols(q_s, scale), v_f, v_s], axis=1).T.astype(BF16)
    bf_row = jnp.pad(b_f, (0, SLAB - N_HEADS)).reshape(1, SLAB)

    kf, ks, qf_t, qs_t, vf_t, vs_t = _inproj(x, scale1, shift1, wtok, wfeat, bf_row, _placement(),
                                             t=t_proj, tk=t_attn)
    of_t = _attention(_fox_kernel, qf_t, kf, vf_t, fox_g.reshape(-1, 1),
                      tq=t_attn, heads=attn_heads, name="fox")
    os_t = _attention(_sb_kernel, qs_t, ks, vs_t, sb_g.reshape(-1, 1),
                      tq=t_attn, heads=attn_heads, name="sb")

    x1, u_packed, logits_t = _outproj(
        of_t, os_t, x, gate1, scale2, shift2,
        w_out[:d_val].astype(BF16), w_out[d_val:].astype(BF16),
        ln1_g.reshape(1, d), ln1_b.reshape(1, d), w_router.T, t=t_proj, alpha=alpha)

    idx_t, rank_t, w_t, counts = _route(logits_t, router_bias.reshape(-1, 1), t=t_route)
    n_blocks = -(-(n * TOP_K) // MOE_BLOCK) + N_EXPERTS
    dest_t, blk_start, blk_count = _dest(counts, idx_t, rank_t, t=t_route)
    dest_flat = _tile_major(dest_t, t_moe)

    xs = _dispatch(dest_flat, u_packed, jnp.zeros((n_blocks * MOE_BLOCK, d // 2), U32), t=t_moe)
    ys = _experts(blk_start.reshape(-1), blk_count.reshape(-1), xs, w_gate_e, w_up_e, w_down_e)

    wgu_sh = jnp.concatenate([w_gate_sh, w_up_sh], axis=1).astype(BF16)
    yg = _sc_gather(ys, dest_flat).reshape(n // t_moe, TOP_K, t_moe, d // 2)
    out = _combine(yg, w_t.T, x1, gate2, scale2, shift2, wgu_sh, w_down_sh.astype(BF16),
                   ln2_g.reshape(1, d), ln2_b.reshape(1, d), t=t_moe, seq=s, alpha=alpha)
    return out.reshape(b, s, d)


def kernel(x, c, w_ada, b_ada, w_in, b_f, fox_norm_g, sb_norm_g, w_out, ln1_g, ln1_b, w_router, router_bias, w_gate_e, w_up_e, w_down_e, w_gate_sh, w_up_sh, w_down_sh, ln2_g, ln2_b):
    depth = w_ada.shape[0]
    alpha = (2.0 * depth) ** 0.25
    c_pad = jnp.pad(c, ((0, (-c.shape[0]) % 8), (0, 0)))
    for l in range(depth):
        x = _layer(x, c_pad, w_ada[l], b_ada[l], w_in[l], b_f[l], fox_norm_g[l], sb_norm_g[l],
                   w_out[l], ln1_g[l], ln1_b[l], w_router[l], router_bias[l],
                   w_gate_e[l], w_up_e[l], w_down_e[l], w_gate_sh[l], w_up_sh[l], w_down_sh[l],
                   ln2_g[l], ln2_b[l], alpha=alpha)
    return x
```

```python
import functools

import jax
import jax.numpy as jnp
from jax import lax
from jax.experimental import pallas as pl
from jax.experimental.pallas import tpu as pltpu
from jax.experimental.pallas import tpu_sc as plsc

F32 = jnp.float32
BF16 = jnp.bfloat16
I32 = jnp.int32
U32 = jnp.uint32

HEAD_DIM = 64
N_HEADS = 8
SLAB = 128
N_EXPERTS = 256
TOP_K = 8
N_GROUPS = 8
TOP_GROUPS = 4
GROUP_SIZE = N_EXPERTS // N_GROUPS
ROUTED_SCALE = 2.5
MOE_BLOCK = 256
MOE_SPLIT = 2
SB_CHUNK = 128
LN_EPS = 1e-5
RMS_EPS = 1e-6
NEG_BIG = -1e30
LOG2_E = 1.4426950408889634

VMEM_LIMIT = 56 * 1024 * 1024


def _cparams(sem):
    return pltpu.CompilerParams(dimension_semantics=sem, vmem_limit_bytes=VMEM_LIMIT)


def _dot(a, b):
    return jnp.dot(a, b, preferred_element_type=F32)


def _dot_nt(a, b):
    return lax.dot_general(a, b, (((1,), (1,)), ((), ())), preferred_element_type=F32)


def _dot_tn(a, b):
    return lax.dot_general(a, b, (((0,), (0,)), ((), ())), preferred_element_type=F32)


def _split2(x):
    hi = x.astype(BF16)
    lo = (x - hi.astype(F32)).astype(BF16)
    return hi, lo


def _split3(x):
    p1 = x.astype(BF16)
    r1 = x - p1.astype(F32)
    p2 = r1.astype(BF16)
    p3 = (r1 - p2.astype(F32)).astype(BF16)
    return p1, p2, p3


def _dot3(a, b):
    a_hi, a_lo = _split2(a)
    b_hi, b_lo = _split2(b)
    return _dot(a_hi, b_hi) + _dot(a_lo, b_hi) + _dot(a_hi, b_lo)


def _dot3_nt(a, b):
    a_hi, a_lo = _split2(a)
    b_hi, b_lo = _split2(b)
    return _dot_nt(a_hi, b_hi) + _dot_nt(a_lo, b_hi) + _dot_nt(a_hi, b_lo)


def _pack(x):
    half = x.shape[1] // 2
    lo = lax.bitcast_convert_type(x[:, :half].astype(BF16).astype(F32), U32)
    hi = lax.bitcast_convert_type(x[:, half:].astype(BF16).astype(F32), U32)
    return (lo >> 16) | (hi & jnp.uint32(0xFFFF0000))


def _unpack_f32(words):
    lo = lax.bitcast_convert_type(words << 16, F32)
    hi = lax.bitcast_convert_type(words & jnp.uint32(0xFFFF0000), F32)
    return lo, hi


def _softplus(z):
    return jnp.maximum(z, 0.0) + jnp.log(1.0 + jnp.exp(-jnp.abs(z)))


def _layer_norm(v, g, b):
    mu = jnp.mean(v, axis=-1, keepdims=True)
    d = v - mu
    var = jnp.mean(d * d, axis=-1, keepdims=True)
    return d * lax.rsqrt(var + LN_EPS) * g + b


def _ada_kernel(c_ref, w_ref, b_ref, o_ref):
    c = c_ref[...]
    s = c * jax.nn.sigmoid(c)
    o_ref[...] = _dot3(s, w_ref[...]) + b_ref[...]


def _ada(c_pad, w, b):
    rows, d = c_pad.shape
    n = w.shape[1]
    tn = 1024
    return pl.pallas_call(
        _ada_kernel,
        out_shape=jax.ShapeDtypeStruct((rows, n), F32),
        grid=(n // tn,),
        in_specs=[pl.BlockSpec((rows, d), lambda j: (0, 0)),
                  pl.BlockSpec((d, tn), lambda j: (0, j)),
                  pl.BlockSpec((1, tn), lambda j: (0, j))],
        out_specs=pl.BlockSpec((rows, tn), lambda j: (0, j)),
        compiler_params=_cparams(("arbitrary",)),
        name="ada",
    )(c_pad, w, b)


def _inproj_kernel(x_ref, sc_ref, sh_ref, wtok_ref, wfeat_ref, bf_ref, place_ref,
                   kf_ref, ks_ref, qf_ref, qs_ref, vf_ref, vs_ref, carry_ref, *, t, tk):
    d_slab = N_HEADS * SLAB
    d_val = N_HEADS * HEAD_DIM

    @pl.when(pl.program_id(1) == 0)
    def _():
        carry_ref[...] = jnp.zeros_like(carry_ref)

    u = x_ref[0] * (1.0 + sc_ref[0]) + sh_ref[0]
    ub = u.astype(BF16)

    flog = _dot(ub, wtok_ref[:, 2 * d_slab:]) + bf_ref[...]
    logf = -_softplus(-flog)
    lane = lax.broadcasted_iota(I32, logf.shape, 1)
    logf = jnp.where(lane < N_HEADS, logf, 0.0)
    row = lax.broadcasted_iota(I32, (t, t), 0)
    col = lax.broadcasted_iota(I32, (t, t), 1)
    tri = jnp.where(row >= col, 1.0, 0.0).astype(BF16)
    lf_hi, lf_lo = _split2(logf)
    cum = _dot(tri, lf_hi) + _dot(tri, lf_lo) + carry_ref[...]
    carry_ref[...] = cum[t - 1:t, :]
    n1, n2, n3 = _split3(-LOG2_E * cum)
    extras = _dot(n1, place_ref[0]) + _dot(n2, place_ref[1]) + _dot(n3, place_ref[2])

    kf_ref[0] = (_dot(ub, wtok_ref[:, :d_slab]) + extras).astype(BF16)
    ks_ref[0] = _dot(ub, wtok_ref[:, d_slab:2 * d_slab]).astype(BF16)

    qf = _dot_nt(wfeat_ref[:d_slab, :], ub)
    r = lax.broadcasted_iota(I32, qf.shape, 0) % SLAB
    qf = jnp.where((r >= HEAD_DIM) & (r < HEAD_DIM + 3), 1.0, qf)
    qf_ref[0] = qf.astype(BF16)
    qs_ref[0] = _dot_nt(wfeat_ref[d_slab:2 * d_slab, :], ub).astype(BF16)
    vf = _dot_nt(wfeat_ref[2 * d_slab:2 * d_slab + d_val, :], ub).astype(BF16)
    vs = _dot_nt(wfeat_ref[2 * d_slab + d_val:, :], ub).astype(BF16)
    for c in range(t // tk):
        vf_ref[0, c] = vf[:, c * tk:(c + 1) * tk]
        vs_ref[0, c] = vs[:, c * tk:(c + 1) * tk]


def _inproj(x, sc, sh, wtok, wfeat, bf_row, place, *, t, tk):
    b, s, d = x.shape
    d_slab = N_HEADS * SLAB
    d_val = N_HEADS * HEAD_DIM
    nt = s // t
    out_shape = (
        jax.ShapeDtypeStruct((b, s, d_slab), BF16),
        jax.ShapeDtypeStruct((b, s, d_slab), BF16),
        jax.ShapeDtypeStruct((b, d_slab, s), BF16),
        jax.ShapeDtypeStruct((b, d_slab, s), BF16),
        jax.ShapeDtypeStruct((b, s // tk, d_val, tk), BF16),
        jax.ShapeDtypeStruct((b, s // tk, d_val, tk), BF16),
    )
    tok_spec = pl.BlockSpec((1, t, d_slab), lambda i, j: (i, j, 0))
    feat_spec = pl.BlockSpec((1, d_slab, t), lambda i, j: (i, 0, j))
    val_spec = pl.BlockSpec((1, t // tk, d_val, tk), lambda i, j: (i, j, 0, 0))
    return pl.pallas_call(
        functools.partial(_inproj_kernel, t=t, tk=tk),
        out_shape=out_shape,
        grid=(b, nt),
        in_specs=[pl.BlockSpec((1, t, d), lambda i, j: (i, j, 0)),
                  pl.BlockSpec((1, 1, d), lambda i, j: (i, 0, 0)),
                  pl.BlockSpec((1, 1, d), lambda i, j: (i, 0, 0)),
                  pl.BlockSpec(wtok.shape, lambda i, j: (0, 0)),
                  pl.BlockSpec(wfeat.shape, lambda i, j: (0, 0)),
                  pl.BlockSpec((1, SLAB), lambda i, j: (0, 0)),
                  pl.BlockSpec(place.shape, lambda i, j: (0, 0, 0))],
        out_specs=(tok_spec, tok_spec, feat_spec, feat_spec, val_spec, val_spec),
        scratch_shapes=[pltpu.VMEM((1, SLAB), F32)],
        compiler_params=_cparams(("arbitrary", "arbitrary")),
        name="inproj",
    )(x, sc, sh, wtok, wfeat, bf_row, place)


def _head_norm_store(acc, g_ref, o_ref, h):
    rows = slice(h * HEAD_DIM, (h + 1) * HEAD_DIM)
    ms = jnp.mean(acc * acc, axis=0, keepdims=True)
    o_ref[0, rows, :] = (acc * lax.rsqrt(ms + RMS_EPS) * g_ref[rows, :]).astype(BF16)


def _fox_kernel(q_ref, k_ref, v_ref, g_ref, o_ref, *, tq, heads):
    qi = pl.program_id(2)

    def tile(kb, carry, masked):
        start = pl.multiple_of(kb * tq, tq)
        k_all = k_ref[0, pl.ds(start, tq), :]
        v_all = v_ref[0, kb]
        scores = [_dot(k_all[:, h * SLAB:(h + 1) * SLAB], q_ref[0, h * SLAB:(h + 1) * SLAB, :])
                  for h in range(heads)]
        out = []
        for h in range(heads):
            m, l, acc = carry[h]
            s = scores[h]
            if masked:
                kid = lax.broadcasted_iota(I32, s.shape, 0)
                qid = lax.broadcasted_iota(I32, s.shape, 1)
                s = jnp.where(kid <= qid, s, NEG_BIG)
            m_new = jnp.maximum(m, jnp.max(s, axis=0, keepdims=True))
            p = jnp.exp2(s - m_new)
            alpha = jnp.exp2(m - m_new)
            l = alpha * l + jnp.sum(p, axis=0, keepdims=True)
            v = v_all[h * HEAD_DIM:(h + 1) * HEAD_DIM, :]
            acc = alpha * acc + _dot(v, p.astype(BF16))
            out.append((m_new, l, acc))
        return tuple(out)

    init = tuple((jnp.full((1, tq), NEG_BIG, F32), jnp.zeros((1, tq), F32),
                  jnp.zeros((HEAD_DIM, tq), F32)) for _ in range(heads))
    carry = lax.fori_loop(0, qi, lambda kb, c: tile(kb, c, False), init)
    final = tile(qi, carry, True)
    for h in range(heads):
        _, l, acc = final[h]
        _head_norm_store(acc / l, g_ref, o_ref, h)


def _sb_kernel(q_ref, k_ref, v_ref, g_ref, o_ref, *, tq, heads):
    qi = pl.program_id(2)
    cb = SB_CHUNK
    nchunk = tq // cb
    r = lax.broadcasted_iota(I32, (cb + 16, 2 * cb), 0)
    c = lax.broadcasted_iota(I32, (cb + 16, 2 * cb), 1) % cb
    uu = jnp.where(((r < cb) & (c > r)) | (r == cb), 1.0, 0.0).astype(BF16)
    row = lax.broadcasted_iota(I32, (tq, tq), 0)
    col = lax.broadcasted_iota(I32, (tq, tq), 1)
    past = row < col

    def tile(kb, carry, masked):
        start = pl.multiple_of(kb * tq, tq)
        k_all = k_ref[0, pl.ds(start, tq), :]
        v_all = v_ref[0, kb]
        zs = [_dot(k_all[:, h * SLAB:(h + 1) * SLAB], q_ref[0, h * SLAB:(h + 1) * SLAB, :])
              for h in range(heads)]
        lbs, exts = [], []
        for h in range(heads):
            z = zs[h]
            sp = jnp.maximum(z, 0.0) + jnp.log2(1.0 + jnp.exp2(-jnp.abs(z)))
            lbs.append(z - sp)
            if masked:
                sp = jnp.where(past, sp, 0.0)
            hi, lo = _split2(sp)
            exts.append([_dot(uu, jnp.concatenate([hi[ci * cb:(ci + 1) * cb, :],
                                                   lo[ci * cb:(ci + 1) * cb, :]], axis=0))
                         for ci in range(nchunk)])
        out = []
        for h in range(heads):
            run, acc = carry[h]
            chunks = [None] * nchunk
            for ci in reversed(range(nchunk)):
                rows = slice(ci * cb, (ci + 1) * cb)
                a = jnp.exp2(lbs[h][rows, :] - exts[h][ci][:cb, :] - run)
                if masked:
                    a = jnp.where(past[rows, :], a, 0.0)
                chunks[ci] = a.astype(BF16)
                run = run + exts[h][ci][cb:cb + 1, :]
            v = v_all[h * HEAD_DIM:(h + 1) * HEAD_DIM, :]
            acc = acc + _dot(v, jnp.concatenate(chunks, axis=0))
            out.append((run, acc))
        return tuple(out)

    init = tuple((jnp.zeros((1, tq), F32), jnp.zeros((HEAD_DIM, tq), F32)) for _ in range(heads))
    carry = tile(qi, init, True)
    final = lax.fori_loop(0, qi, lambda i, cr: tile(qi - 1 - i, cr, False), carry)
    for h in range(heads):
        _head_norm_store(final[h][1], g_ref, o_ref, h)


def _attention(kernel, qt, k, vt, g_col, *, tq, heads, name):
    b, d_slab, s = qt.shape
    nk = vt.shape[1]
    d_val = N_HEADS * HEAD_DIM
    return pl.pallas_call(
        functools.partial(kernel, tq=tq, heads=heads),
        out_shape=jax.ShapeDtypeStruct((b, d_val, s), BF16),
        grid=(b, N_HEADS // heads, s // tq),
        in_specs=[pl.BlockSpec((1, heads * SLAB, tq), lambda i, h, j: (i, h, j)),
                  pl.BlockSpec((1, s, heads * SLAB), lambda i, h, j: (i, 0, h)),
                  pl.BlockSpec((1, nk, heads * HEAD_DIM, tq), lambda i, h, j: (i, 0, h, 0)),
                  pl.BlockSpec((heads * HEAD_DIM, 1), lambda i, h, j: (h, 0))],
        out_specs=pl.BlockSpec((1, heads * HEAD_DIM, tq), lambda i, h, j: (i, h, j)),
        compiler_params=_cparams(("arbitrary", "arbitrary", "arbitrary")),
        name=name,
    )(qt, k, vt, g_col)


def _outproj_kernel(of_ref, os_ref, x_ref, gate_ref, sc_ref, sh_ref, wa_ref, wb_ref,
                    g_ref, b_ref, wr_ref, x1_ref, up_ref, lg_ref, *, alpha):
    d = x_ref.shape[2]
    mix = _dot_tn(of_ref[0], wa_ref[...]) + _dot_tn(os_ref[0], wb_ref[...])
    x1 = _layer_norm(alpha * x_ref[0] + (1.0 + gate_ref[0]) * mix, g_ref[...], b_ref[...])
    x1_ref[...] = x1
    u2 = x1 * (1.0 + sc_ref[0]) + sh_ref[0]
    up_ref[...] = _pack(u2)
    lg_ref[...] = _dot3_nt(wr_ref[...], u2)


def _outproj(of_t, os_t, x, gate, sc, sh, wa, wb, g, bb, wr_t, *, t, alpha):
    b, s, d = x.shape
    n = b * s
    nt = s // t
    e = wr_t.shape[0]
    d_val = of_t.shape[1]
    mod_spec = pl.BlockSpec((1, 1, d), lambda i, j: (i, 0, 0))
    return pl.pallas_call(
        functools.partial(_outproj_kernel, alpha=alpha),
        out_shape=(jax.ShapeDtypeStruct((n, d), F32),
                   jax.ShapeDtypeStruct((n, d // 2), U32),
                   jax.ShapeDtypeStruct((e, n), F32)),
        grid=(b, nt),
        in_specs=[pl.BlockSpec((1, d_val, t), lambda i, j: (i, 0, j)),
                  pl.BlockSpec((1, d_val, t), lambda i, j: (i, 0, j)),
                  pl.BlockSpec((1, t, d), lambda i, j: (i, j, 0)),
                  mod_spec, mod_spec, mod_spec,
                  pl.BlockSpec(wa.shape, lambda i, j: (0, 0)),
                  pl.BlockSpec(wb.shape, lambda i, j: (0, 0)),
                  pl.BlockSpec((1, d), lambda i, j: (0, 0)),
                  pl.BlockSpec((1, d), lambda i, j: (0, 0)),
                  pl.BlockSpec(wr_t.shape, lambda i, j: (0, 0))],
        out_specs=(pl.BlockSpec((t, d), lambda i, j: (i * nt + j, 0)),
                   pl.BlockSpec((t, d // 2), lambda i, j: (i * nt + j, 0)),
                   pl.BlockSpec((e, t), lambda i, j: (0, i * nt + j))),
        compiler_params=_cparams(("arbitrary", "arbitrary")),
        name="outproj",
    )(of_t, os_t, x, gate, sc, sh, wa, wb, g, bb, wr_t)


def _route_kernel(lg_ref, bias_ref, idx_ref, rank_ref, w_ref, cnt_ref, *, t):
    @pl.when(pl.program_id(0) == 0)
    def _():
        cnt_ref[...] = jnp.zeros_like(cnt_ref)

    scores = jax.nn.sigmoid(lg_ref[...])
    sel = scores + bias_ref[...]
    neg_inf = -jnp.inf

    gscore = []
    for gi in range(N_GROUPS):
        blk = sel[gi * GROUP_SIZE:(gi + 1) * GROUP_SIZE, :]
        m1 = jnp.max(blk, axis=0, keepdims=True)
        is_max = blk == m1
        n_max = jnp.sum(jnp.where(is_max, 1.0, 0.0), axis=0, keepdims=True)
        m2 = jnp.max(jnp.where(is_max, neg_inf, blk), axis=0, keepdims=True)
        gscore.append(m1 + jnp.where(n_max >= 2.0, m1, m2))

    parts = []
    for gi in range(N_GROUPS):
        beaten = jnp.zeros_like(gscore[gi])
        for gj in range(N_GROUPS):
            if gj == gi:
                continue
            wins = (gscore[gj] > gscore[gi]) if gj > gi else (gscore[gj] >= gscore[gi])
            beaten = beaten + jnp.where(wins, 1.0, 0.0)
        keep = beaten < float(TOP_GROUPS)
        blk = sel[gi * GROUP_SIZE:(gi + 1) * GROUP_SIZE, :]
        parts.append(jnp.where(keep, blk, neg_inf))
    cand = jnp.concatenate(parts, axis=0)

    eidx = lax.broadcasted_iota(I32, cand.shape, 0).astype(F32)
    chosen = jnp.zeros(cand.shape, F32)
    idx_rows, w_rows = [], []
    for _ in range(TOP_K):
        mx = jnp.max(cand, axis=0, keepdims=True)
        idx = jnp.min(jnp.where(cand == mx, eidx, float(N_EXPERTS)), axis=0, keepdims=True)
        hit = eidx == idx
        w_rows.append(jnp.sum(jnp.where(hit, scores, 0.0), axis=0, keepdims=True))
        idx_rows.append(idx)
        cand = jnp.where(hit, neg_inf, cand)
        chosen = jnp.where(hit, 1.0, chosen)

    w_sum = w_rows[0]
    for wk in w_rows[1:]:
        w_sum = w_sum + wk

    r = lax.broadcasted_iota(I32, (t, t), 0)
    c = lax.broadcasted_iota(I32, (t, t), 1)
    before = jnp.where(r < c, 1.0, 0.0).astype(BF16)
    prefix = _dot(chosen.astype(BF16), before) + cnt_ref[...]
    for k in range(TOP_K):
        hit = eidx == idx_rows[k]
        rank = jnp.sum(jnp.where(hit, prefix, 0.0), axis=0, keepdims=True)
        idx_ref[k:k + 1, :] = idx_rows[k].astype(I32)
        rank_ref[k:k + 1, :] = rank.astype(I32)
        w_ref[k:k + 1, :] = w_rows[k] / w_sum * ROUTED_SCALE
    cnt_ref[...] = cnt_ref[...] + jnp.sum(chosen, axis=1, keepdims=True)


def _route(logits_t, bias_col, *, t):
    e, n = logits_t.shape
    row_spec = pl.BlockSpec((TOP_K, t), lambda i: (0, i))
    return pl.pallas_call(
        functools.partial(_route_kernel, t=t),
        out_shape=(jax.ShapeDtypeStruct((TOP_K, n), I32),
                   jax.ShapeDtypeStruct((TOP_K, n), I32),
                   jax.ShapeDtypeStruct((TOP_K, n), F32),
                   jax.ShapeDtypeStruct((e, 1), F32)),
        grid=(n // t,),
        in_specs=[pl.BlockSpec((e, t), lambda i: (0, i)),
                  pl.BlockSpec((e, 1), lambda i: (0, 0))],
        out_specs=(row_spec, row_spec, row_spec, pl.BlockSpec((e, 1), lambda i: (0, 0))),
        compiler_params=_cparams(("arbitrary",)),
        name="route",
    )(logits_t, bias_col)


def _dest_kernel(cnt_ref, idx_ref, rank_ref, dest_ref, bs_ref, nb_ref, pstart_ref):
    e = cnt_ref.shape[0]

    @pl.when(pl.program_id(0) == 0)
    def _():
        nblk = jnp.floor((cnt_ref[...] + float(MOE_BLOCK - 1)) * (1.0 / MOE_BLOCK))
        hi = jnp.floor(nblk * (1.0 / 32.0))
        lo = nblk - 32.0 * hi
        r = lax.broadcasted_iota(I32, (e, e), 0)
        c = lax.broadcasted_iota(I32, (e, e), 1)
        below = jnp.where(c < r, 1.0, 0.0).astype(BF16)
        hi_b = jnp.broadcast_to(hi, (e, SLAB)).astype(BF16)
        lo_b = jnp.broadcast_to(lo, (e, SLAB)).astype(BF16)
        bstart = 32.0 * _dot(below, hi_b) + _dot(below, lo_b)
        pstart_ref[...] = bstart[:, 0:1] * float(MOE_BLOCK)
        bs_ref[...] = bstart[:, 0:1].astype(I32)
        nb_ref[...] = nblk.astype(I32)

    eidx = lax.broadcasted_iota(I32, (e, idx_ref.shape[1]), 0)
    for k in range(TOP_K):
        hit = eidx == idx_ref[k:k + 1, :]
        base = jnp.sum(jnp.where(hit, pstart_ref[...], 0.0), axis=0, keepdims=True)
        dest_ref[k:k + 1, :] = base.astype(I32) + rank_ref[k:k + 1, :]


def _dest(counts, idx_t, rank_t, *, t):
    e = counts.shape[0]
    n = idx_t.shape[1]
    row_spec = pl.BlockSpec((TOP_K, t), lambda i: (0, i))
    return pl.pallas_call(
        _dest_kernel,
        out_shape=(jax.ShapeDtypeStruct((TOP_K, n), I32),
                   jax.ShapeDtypeStruct((e, 1), I32),
                   jax.ShapeDtypeStruct((e, 1), I32)),
        grid=(n // t,),
        in_specs=[pl.BlockSpec((e, 1), lambda i: (0, 0)), row_spec, row_spec],
        out_specs=(row_spec,
                   pl.BlockSpec((e, 1), lambda i: (0, 0)),
                   pl.BlockSpec((e, 1), lambda i: (0, 0))),
        scratch_shapes=[pltpu.VMEM((e, 1), F32)],
        compiler_params=_cparams(("arbitrary",)),
        name="dest",
    )(counts, idx_t, rank_t)


SC_CORES = 2
SC_SUBCORES = 16
SC_CHUNK = 64


def _sc_scatter_kernel(up_ref, dest_ref, xs_ref, rows, idx, sem, *, t, tiles_per_worker):
    wid = lax.axis_index("s") * SC_CORES + lax.axis_index("c")

    def body(i, _):
        tile = wid * tiles_per_worker + i
        pltpu.sync_copy(up_ref.at[pl.ds(pl.multiple_of(tile * t, t), t)], rows)
        for k in range(TOP_K):
            off = pl.multiple_of((tile * TOP_K + k) * t, t)
            pltpu.sync_copy(dest_ref.at[pl.ds(off, t)], idx.at[k])
        copies = [pltpu.async_copy(rows, xs_ref.at[idx.at[k]], sem) for k in range(TOP_K)]
        for cp in copies:
            cp.wait()
        return 0

    lax.fori_loop(0, tiles_per_worker, body, 0)


def _sc_scatter(up, dest_flat, cap, *, t):
    n, width = up.shape
    workers = SC_CORES * SC_SUBCORES
    tiles_per_worker = n // t // workers
    assert tiles_per_worker * workers * t == n
    mesh = plsc.VectorSubcoreMesh(core_axis_name="c", subcore_axis_name="s")
    return pl.kernel(
        functools.partial(_sc_scatter_kernel, t=t, tiles_per_worker=tiles_per_worker),
        out_type=jax.ShapeDtypeStruct((cap, width), up.dtype),
        mesh=mesh,
        scratch_types=[pltpu.VMEM((t, width), up.dtype), pltpu.VMEM((TOP_K, t), I32), pltpu.SemaphoreType.DMA],
        name="sc_scatter",
    )(up, dest_flat)


def _unpack(words):
    lo, hi = _unpack_f32(words)
    return lo.astype(BF16), hi.astype(BF16)


def _expert_kernel(bs_ref, nb_ref, cnt_ref, xs_ref, wg_ref, wu_ref, wd_ref, ys_ref,
                   wgu_s, wd_s, xbuf, ybuf, xsem, ysem, *, n_blocks, n_experts):
    e = pl.program_id(0)
    last = n_experts - 1
    d_half = xbuf.shape[2]
    d_exp = wg_ref.shape[2]
    nb = nb_ref[e]
    first = bs_ref[e]
    used = bs_ref[last] + nb_ref[last]

    def rows_of(blk):
        return pl.ds(pl.multiple_of(blk * MOE_BLOCK, MOE_BLOCK), MOE_BLOCK)

    def x_copy(blk, slot):
        return pltpu.make_async_copy(xs_ref.at[rows_of(blk)], xbuf.at[slot], xsem.at[slot])

    def y_copy(blk, slot):
        return pltpu.make_async_copy(ybuf.at[slot], ys_ref.at[rows_of(blk)], ysem.at[slot])

    @pl.when((e == 0) & (used > 0))
    def _():
        x_copy(0, 0).start()

    @pl.when(nb > 0)
    def _():
        wgu_s[:, :d_exp] = wg_ref[0].astype(BF16)
        wgu_s[:, d_exp:] = wu_ref[0].astype(BF16)
        wd_s[...] = wd_ref[0].astype(BF16)

        def block(g, _):
            slot = lax.rem(g, 2)
            x_copy(g, slot).wait()

            @pl.when(g + 1 < used)
            def _():
                x_copy(g + 1, 1 - slot).start()

            @pl.when(g >= 2)
            def _():
                y_copy(g - 2, slot).wait()

            rows = MOE_BLOCK // MOE_SPLIT
            for part in range(MOE_SPLIT):
                sl = slice(part * rows, (part + 1) * rows)
                row = lax.broadcasted_iota(I32, (rows, d_half), 0) + (part * rows)
                live = row < cnt_ref[e] - (g - first) * MOE_BLOCK
                lo, hi = _unpack(jnp.where(live, xbuf[slot, sl, :], jnp.uint32(0)))
                gu = _dot(lo, wgu_s[:d_half, :]) + _dot(hi, wgu_s[d_half:, :])
                gate = gu[:, :d_exp]
                h = gate * jax.nn.sigmoid(gate) * gu[:, d_exp:]
                ybuf[slot, sl, :] = _pack(_dot(h.astype(BF16), wd_s[...]))
            y_copy(g, slot).start()
            return 0

        lax.fori_loop(first, first + nb, block, 0)

    @pl.when(e == last)
    def _():
        @pl.when(used >= 2)
        def _():
            y_copy(used - 2, lax.rem(used, 2)).wait()

        @pl.when(used >= 1)
        def _():
            y_copy(used - 1, lax.rem(used - 1, 2)).wait()

        ybuf[0] = jnp.zeros(ybuf.shape[1:], ybuf.dtype)

        def fill(blk, _):
            y_copy(blk, 0).start()
            return 0

        def done(blk, _):
            y_copy(blk, 0).wait()
            return 0

        lax.fori_loop(used, n_blocks, fill, 0)
        lax.fori_loop(used, n_blocks, done, 0)


def _experts(blk_start, blk_count, counts, xs, wg, wu, wd):
    cap, d_half = xs.shape
    d = 2 * d_half
    n_exp, _, d_exp = wg.shape
    grid_spec = pltpu.PrefetchScalarGridSpec(
        num_scalar_prefetch=3,
        grid=(n_exp,),
        in_specs=[pl.BlockSpec(memory_space=pl.ANY),
                  pl.BlockSpec((1, d, d_exp), lambda i, bs, nb, cnt: (i, 0, 0)),
                  pl.BlockSpec((1, d, d_exp), lambda i, bs, nb, cnt: (i, 0, 0)),
                  pl.BlockSpec((1, d_exp, d), lambda i, bs, nb, cnt: (i, 0, 0))],
        out_specs=pl.BlockSpec(memory_space=pl.ANY),
        scratch_shapes=[pltpu.VMEM((d, 2 * d_exp), BF16), pltpu.VMEM((d_exp, d), BF16),
                        pltpu.VMEM((2, MOE_BLOCK, d_half), U32), pltpu.VMEM((2, MOE_BLOCK, d_half), U32),
                        pltpu.SemaphoreType.DMA((2,)), pltpu.SemaphoreType.DMA((2,))],
    )
    return pl.pallas_call(
        functools.partial(_expert_kernel, n_blocks=cap // MOE_BLOCK, n_experts=n_exp),
        out_shape=jax.ShapeDtypeStruct((cap, d_half), U32),
        grid_spec=grid_spec,
        compiler_params=_cparams(("arbitrary",)),
        name="experts",
    )(blk_start, blk_count, counts, xs, wg, wu, wd)


def _sc_gather_kernel(table_ref, idx_ref, out_ref, idx0, idx1, rows0, rows1, sem0, sem1, *, per_worker):
    wid = lax.axis_index("s") * SC_CORES + lax.axis_index("c")
    base = wid * per_worker

    def body(i, _):
        off0 = pl.multiple_of(base + (2 * i) * SC_CHUNK, SC_CHUNK)
        off1 = pl.multiple_of(off0 + SC_CHUNK, SC_CHUNK)
        pltpu.sync_copy(idx_ref.at[pl.ds(off0, SC_CHUNK)], idx0)
        g0 = pltpu.async_copy(table_ref.at[idx0], rows0, sem0)
        pltpu.sync_copy(idx_ref.at[pl.ds(off1, SC_CHUNK)], idx1)
        g1 = pltpu.async_copy(table_ref.at[idx1], rows1, sem1)
        g0.wait()
        pltpu.sync_copy(rows0, out_ref.at[pl.ds(off0, SC_CHUNK)])
        g1.wait()
        pltpu.sync_copy(rows1, out_ref.at[pl.ds(off1, SC_CHUNK)])
        return 0

    lax.fori_loop(0, per_worker // (2 * SC_CHUNK), body, 0)


def _sc_gather(table, idx):
    rows = idx.shape[0]
    width = table.shape[1]
    workers = SC_CORES * SC_SUBCORES
    per_worker = rows // workers
    assert per_worker * workers == rows and per_worker % (2 * SC_CHUNK) == 0
    mesh = plsc.VectorSubcoreMesh(core_axis_name="c", subcore_axis_name="s")
    return pl.kernel(
        functools.partial(_sc_gather_kernel, per_worker=per_worker),
        out_type=jax.ShapeDtypeStruct((rows, width), table.dtype),
        mesh=mesh,
        scratch_types=[pltpu.VMEM((SC_CHUNK,), I32), pltpu.VMEM((SC_CHUNK,), I32),
                       pltpu.VMEM((SC_CHUNK, width), table.dtype), pltpu.VMEM((SC_CHUNK, width), table.dtype),
                       pltpu.SemaphoreType.DMA, pltpu.SemaphoreType.DMA],
        name="sc_gather",
    )(table, idx)


def _combine_kernel(yg_ref, w_ref, x1_ref, gate_ref, sc_ref, sh_ref,
                    wgu_ref, wd_ref, g_ref, b_ref, o_ref, *, alpha):
    x1 = x1_ref[...]
    ub = (x1 * (1.0 + sc_ref[0]) + sh_ref[0]).astype(BF16)
    d_sh = wd_ref.shape[0]
    gu = _dot(ub, wgu_ref[...])
    gate = gu[:, :d_sh]
    h = gate * jax.nn.sigmoid(gate) * gu[:, d_sh:]
    ffn = _dot(h.astype(BF16), wd_ref[...])

    w = w_ref[...]
    d_half = yg_ref.shape[3]
    lo_sum = ffn[:, :d_half]
    hi_sum = ffn[:, d_half:]
    for k in range(TOP_K):
        lo, hi = _unpack_f32(yg_ref[0, k])
        lo_sum = lo_sum + w[:, k:k + 1] * lo
        hi_sum = hi_sum + w[:, k:k + 1] * hi
    ffn = jnp.concatenate([lo_sum, hi_sum], axis=1)
    o_ref[...] = _layer_norm(alpha * x1 + (1.0 + gate_ref[0]) * ffn, g_ref[...], b_ref[...])


def _combine(yg, w_tok, x1, gate, sc, sh, wgu, wd, g, bb, *, t, seq, alpha):
    n, d = x1.shape
    per_batch = seq // t
    mod_spec = pl.BlockSpec((1, 1, d), lambda i: (i // per_batch, 0, 0))
    return pl.pallas_call(
        functools.partial(_combine_kernel, alpha=alpha),
        out_shape=jax.ShapeDtypeStruct((n, d), F32),
        grid=(n // t,),
        in_specs=[pl.BlockSpec((1, TOP_K, t, d // 2), lambda i: (i, 0, 0, 0)),
                  pl.BlockSpec((t, TOP_K), lambda i: (i, 0)),
                  pl.BlockSpec((t, d), lambda i: (i, 0)),
                  mod_spec, mod_spec, mod_spec,
                  pl.BlockSpec(wgu.shape, lambda i: (0, 0)),
                  pl.BlockSpec(wd.shape, lambda i: (0, 0)),
                  pl.BlockSpec((1, d), lambda i: (0, 0)),
                  pl.BlockSpec((1, d), lambda i: (0, 0))],
        out_specs=pl.BlockSpec((t, d), lambda i: (i, 0)),
        compiler_params=_cparams(("arbitrary",)),
        name="combine",
    )(yg, w_tok, x1, gate, sc, sh, wgu, wd, g, bb)


def _slab_cols(w, scale=1.0):
    d = w.shape[0]
    w = (w * scale).reshape(d, N_HEADS, HEAD_DIM)
    w = jnp.pad(w, ((0, 0), (0, 0), (0, SLAB - HEAD_DIM)))
    return w.reshape(d, N_HEADS * SLAB)


def _placement():
    h = jnp.arange(SLAB)[:, None]
    c = jnp.arange(N_HEADS * SLAB)[None, :]
    mats = [((h < N_HEADS) & (c == h * SLAB + HEAD_DIM + p)) for p in range(3)]
    return jnp.stack(mats).astype(BF16)


def _tile_major(rows, t):
    k, n = rows.shape
    return rows.reshape(k, n // t, t).transpose(1, 0, 2).reshape(-1)


def _layer(x, c_pad, w_ada, b_ada, w_in, b_f, fox_g, sb_g, w_out, ln1_g, ln1_b,
           w_router, router_bias, w_gate_e, w_up_e, w_down_e,
           w_gate_sh, w_up_sh, w_down_sh, ln2_g, ln2_b, *, alpha,
           t_proj=512, t_attn=512, attn_heads=4, t_route=512, t_moe=128):
    b, s, d = x.shape
    n = b * s
    d_val = N_HEADS * HEAD_DIM
    scale = HEAD_DIM ** -0.5 * LOG2_E

    ada = _ada(c_pad, w_ada, b_ada.reshape(1, -1))[:b]
    shift1, scale1, gate1, shift2, scale2, gate2 = [m[:, None, :] for m in jnp.split(ada, 6, axis=-1)]

    q_f, k_f, v_f, q_s, k_s, v_s, w_f = jnp.split(
        w_in, [d_val, 2 * d_val, 3 * d_val, 4 * d_val, 5 * d_val, 6 * d_val], axis=1)
    w_f = jnp.pad(w_f, ((0, 0), (0, SLAB - N_HEADS)))
    wtok = jnp.concatenate([_slab_cols(k_f), _slab_cols(k_s), w_f], axis=1).astype(BF16)
    wfeat = jnp.concatenate([_slab_cols(q_f, scale), _slab_cols(q_s, scale), v_f, v_s], axis=1).T.astype(BF16)
    bf_row = jnp.pad(b_f, (0, SLAB - N_HEADS)).reshape(1, SLAB)

    kf, ks, qf_t, qs_t, vf_t, vs_t = _inproj(x, scale1, shift1, wtok, wfeat, bf_row, _placement(),
                                             t=t_proj, tk=t_attn)
    of_t = _attention(_fox_kernel, qf_t, kf, vf_t, fox_g.reshape(-1, 1),
                      tq=t_attn, heads=attn_heads, name="fox")
    os_t = _attention(_sb_kernel, qs_t, ks, vs_t, sb_g.reshape(-1, 1),
                      tq=t_attn, heads=attn_heads, name="sb")

    x1, u_packed, logits_t = _outproj(
        of_t, os_t, x, gate1, scale2, shift2,
        w_out[:d_val].astype(BF16), w_out[d_val:].astype(BF16),
        ln1_g.reshape(1, d), ln1_b.reshape(1, d), w_router.T, t=t_proj, alpha=alpha)

    idx_t, rank_t, w_t, counts = _route(logits_t, router_bias.reshape(-1, 1), t=t_route)
    n_blocks = -(-(n * TOP_K) // MOE_BLOCK) + N_EXPERTS
    dest_t, blk_start, blk_count = _dest(counts, idx_t, rank_t, t=t_route)
    dest_flat = _tile_major(dest_t, t_moe)

    xs = _sc_scatter(u_packed, dest_flat, n_blocks * MOE_BLOCK, t=t_moe)
    ys = _experts(blk_start.reshape(-1), blk_count.reshape(-1), counts.astype(I32).reshape(-1),
                  xs, w_gate_e, w_up_e, w_down_e)

    wgu_sh = jnp.concatenate([w_gate_sh, w_up_sh], axis=1).astype(BF16)
    yg = _sc_gather(ys, dest_flat).reshape(n // t_moe, TOP_K, t_moe, d // 2)
    out = _combine(yg, w_t.T, x1, gate2, scale2, shift2, wgu_sh, w_down_sh.astype(BF16),
                   ln2_g.reshape(1, d), ln2_b.reshape(1, d), t=t_moe, seq=s, alpha=alpha)
    return out.reshape(b, s, d)


def kernel(x, c, w_ada, b_ada, w_in, b_f, fox_norm_g, sb_norm_g, w_out, ln1_g, ln1_b, w_router, router_bias, w_gate_e, w_up_e, w_down_e, w_gate_sh, w_up_sh, w_down_sh, ln2_g, ln2_b):
    depth = w_ada.shape[0]
    alpha = (2.0 * depth) ** 0.25
    c_pad = jnp.pad(c, ((0, (-c.shape[0]) % 8), (0, 0)))
    for l in range(depth):
        x = _layer(x, c_pad, w_ada[l], b_ada[l], w_in[l], b_f[l], fox_norm_g[l], sb_norm_g[l],
                   w_out[l], ln1_g[l], ln1_b[l], w_router[l], router_bias[l],
                   w_gate_e[l], w_up_e[l], w_down_e[l], w_gate_sh[l], w_up_sh[l], w_down_sh[l],
                   ln2_g[l], ln2_b[l], alpha=alpha)
    return x
```

```python
import functools

import jax
import jax.numpy as jnp
from jax import lax
from jax.experimental import pallas as pl
from jax.experimental.pallas import tpu as pltpu
from jax.experimental.pallas import tpu_sc as plsc

F32 = jnp.float32
BF16 = jnp.bfloat16
I32 = jnp.int32
U32 = jnp.uint32

HEAD_DIM = 64
N_HEADS = 8
SLAB = 128
N_EXPERTS = 256
TOP_K = 8
N_GROUPS = 8
TOP_GROUPS = 4
GROUP_SIZE = N_EXPERTS // N_GROUPS
ROUTED_SCALE = 2.5
MOE_BLOCK = 256
MOE_SPLIT = 2
SB_CHUNK = 128
LN_EPS = 1e-5
RMS_EPS = 1e-6
NEG_BIG = -1e30
LOG2_E = 1.4426950408889634

VMEM_LIMIT = 56 * 1024 * 1024


def _cparams(sem):
    return pltpu.CompilerParams(dimension_semantics=sem, vmem_limit_bytes=VMEM_LIMIT)


def _dot(a, b):
    return jnp.dot(a, b, preferred_element_type=F32)


def _dot_nt(a, b):
    return lax.dot_general(a, b, (((1,), (1,)), ((), ())), preferred_element_type=F32)


def _dot_tn(a, b):
    return lax.dot_general(a, b, (((0,), (0,)), ((), ())), preferred_element_type=F32)


def _split2(x):
    hi = x.astype(BF16)
    lo = (x - hi.astype(F32)).astype(BF16)
    return hi, lo


def _split3(x):
    p1 = x.astype(BF16)
    r1 = x - p1.astype(F32)
    p2 = r1.astype(BF16)
    p3 = (r1 - p2.astype(F32)).astype(BF16)
    return p1, p2, p3


def _dot3(a, b):
    a_hi, a_lo = _split2(a)
    b_hi, b_lo = _split2(b)
    return _dot(a_hi, b_hi) + _dot(a_lo, b_hi) + _dot(a_hi, b_lo)


def _dot3_nt(a, b):
    a_hi, a_lo = _split2(a)
    b_hi, b_lo = _split2(b)
    return _dot_nt(a_hi, b_hi) + _dot_nt(a_lo, b_hi) + _dot_nt(a_hi, b_lo)


def _pack(x):
    half = x.shape[1] // 2
    lo = lax.bitcast_convert_type(x[:, :half].astype(BF16).astype(F32), U32)
    hi = lax.bitcast_convert_type(x[:, half:].astype(BF16).astype(F32), U32)
    return (lo >> 16) | (hi & jnp.uint32(0xFFFF0000))


def _unpack_f32(words):
    lo = lax.bitcast_convert_type(words << 16, F32)
    hi = lax.bitcast_convert_type(words & jnp.uint32(0xFFFF0000), F32)
    return lo, hi


def _softplus(z):
    return jnp.maximum(z, 0.0) + jnp.log(1.0 + jnp.exp(-jnp.abs(z)))


def _layer_norm(v, g, b):
    mu = jnp.mean(v, axis=-1, keepdims=True)
    d = v - mu
    var = jnp.mean(d * d, axis=-1, keepdims=True)
    return d * lax.rsqrt(var + LN_EPS) * g + b


def _ada_kernel(c_ref, w_ref, b_ref, o_ref):
    c = c_ref[...]
    s = c * jax.nn.sigmoid(c)
    o_ref[...] = _dot3(s, w_ref[...]) + b_ref[...]


def _ada(c_pad, w, b):
    rows, d = c_pad.shape
    n = w.shape[1]
    tn = 1024
    return pl.pallas_call(
        _ada_kernel,
        out_shape=jax.ShapeDtypeStruct((rows, n), F32),
        grid=(n // tn,),
        in_specs=[pl.BlockSpec((rows, d), lambda j: (0, 0)),
                  pl.BlockSpec((d, tn), lambda j: (0, j)),
                  pl.BlockSpec((1, tn), lambda j: (0, j))],
        out_specs=pl.BlockSpec((rows, tn), lambda j: (0, j)),
        compiler_params=_cparams(("arbitrary",)),
        name="ada",
    )(c_pad, w, b)


def _inproj_kernel(x_ref, sc_ref, sh_ref, wtok_ref, wfeat_ref, bf_ref, place_ref,
                   kf_ref, ks_ref, qf_ref, qs_ref, vf_ref, vs_ref, carry_ref, *, t, tk):
    d_slab = N_HEADS * SLAB
    d_val = N_HEADS * HEAD_DIM

    @pl.when(pl.program_id(1) == 0)
    def _():
        carry_ref[...] = jnp.zeros_like(carry_ref)

    u = x_ref[0] * (1.0 + sc_ref[0]) + sh_ref[0]
    ub = u.astype(BF16)

    flog = _dot(ub, wtok_ref[:, 2 * d_slab:]) + bf_ref[...]
    logf = -_softplus(-flog)
    lane = lax.broadcasted_iota(I32, logf.shape, 1)
    logf = jnp.where(lane < N_HEADS, logf, 0.0)
    row = lax.broadcasted_iota(I32, (t, t), 0)
    col = lax.broadcasted_iota(I32, (t, t), 1)
    tri = jnp.where(row >= col, 1.0, 0.0).astype(BF16)
    lf_hi, lf_lo = _split2(logf)
    cum = _dot(tri, lf_hi) + _dot(tri, lf_lo) + carry_ref[...]
    carry_ref[...] = cum[t - 1:t, :]
    n1, n2, n3 = _split3(-LOG2_E * cum)
    extras = _dot(n1, place_ref[0]) + _dot(n2, place_ref[1]) + _dot(n3, place_ref[2])

    kf_ref[0] = (_dot(ub, wtok_ref[:, :d_slab]) + extras).astype(BF16)
    ks_ref[0] = _dot(ub, wtok_ref[:, d_slab:2 * d_slab]).astype(BF16)

    qf = _dot_nt(wfeat_ref[:d_slab, :], ub)
    r = lax.broadcasted_iota(I32, qf.shape, 0) % SLAB
    qf = jnp.where((r >= HEAD_DIM) & (r < HEAD_DIM + 3), 1.0, qf)
    qf_ref[0] = qf.astype(BF16)
    qs_ref[0] = _dot_nt(wfeat_ref[d_slab:2 * d_slab, :], ub).astype(BF16)
    vf = _dot_nt(wfeat_ref[2 * d_slab:2 * d_slab + d_val, :], ub).astype(BF16)
    vs = _dot_nt(wfeat_ref[2 * d_slab + d_val:, :], ub).astype(BF16)
    for c in range(t // tk):
        vf_ref[0, c] = vf[:, c * tk:(c + 1) * tk]
        vs_ref[0, c] = vs[:, c * tk:(c + 1) * tk]


def _inproj(x, sc, sh, wtok, wfeat, bf_row, place, *, t, tk):
    b, s, d = x.shape
    d_slab = N_HEADS * SLAB
    d_val = N_HEADS * HEAD_DIM
    nt = s // t
    out_shape = (
        jax.ShapeDtypeStruct((b, s, d_slab), BF16),
        jax.ShapeDtypeStruct((b, s, d_slab), BF16),
        jax.ShapeDtypeStruct((b, d_slab, s), BF16),
        jax.ShapeDtypeStruct((b, d_slab, s), BF16),
        jax.ShapeDtypeStruct((b, s // tk, d_val, tk), BF16),
        jax.ShapeDtypeStruct((b, s // tk, d_val, tk), BF16),
    )
    tok_spec = pl.BlockSpec((1, t, d_slab), lambda i, j: (i, j, 0))
    feat_spec = pl.BlockSpec((1, d_slab, t), lambda i, j: (i, 0, j))
    val_spec = pl.BlockSpec((1, t // tk, d_val, tk), lambda i, j: (i, j, 0, 0))
    return pl.pallas_call(
        functools.partial(_inproj_kernel, t=t, tk=tk),
        out_shape=out_shape,
        grid=(b, nt),
        in_specs=[pl.BlockSpec((1, t, d), lambda i, j: (i, j, 0)),
                  pl.BlockSpec((1, 1, d), lambda i, j: (i, 0, 0)),
                  pl.BlockSpec((1, 1, d), lambda i, j: (i, 0, 0)),
                  pl.BlockSpec(wtok.shape, lambda i, j: (0, 0)),
                  pl.BlockSpec(wfeat.shape, lambda i, j: (0, 0)),
                  pl.BlockSpec((1, SLAB), lambda i, j: (0, 0)),
                  pl.BlockSpec(place.shape, lambda i, j: (0, 0, 0))],
        out_specs=(tok_spec, tok_spec, feat_spec, feat_spec, val_spec, val_spec),
        scratch_shapes=[pltpu.VMEM((1, SLAB), F32)],
        compiler_params=_cparams(("arbitrary", "arbitrary")),
        name="inproj",
    )(x, sc, sh, wtok, wfeat, bf_row, place)


def _head_norm_store(acc, g_ref, o_ref, h):
    rows = slice(h * HEAD_DIM, (h + 1) * HEAD_DIM)
    ms = jnp.mean(acc * acc, axis=0, keepdims=True)
    o_ref[0, rows, :] = (acc * lax.rsqrt(ms + RMS_EPS) * g_ref[rows, :]).astype(BF16)


def _fox_kernel(q_ref, k_ref, v_ref, g_ref, o_ref, *, tq, heads):
    qi = pl.program_id(2)

    def tile(kb, carry, masked):
        start = pl.multiple_of(kb * tq, tq)
        k_all = k_ref[0, pl.ds(start, tq), :]
        v_all = v_ref[0, kb]
        scores = [_dot(k_all[:, h * SLAB:(h + 1) * SLAB], q_ref[0, h * SLAB:(h + 1) * SLAB, :])
                  for h in range(heads)]
        out = []
        for h in range(heads):
            m, l, acc = carry[h]
            s = scores[h]
            if masked:
                kid = lax.broadcasted_iota(I32, s.shape, 0)
                qid = lax.broadcasted_iota(I32, s.shape, 1)
                s = jnp.where(kid <= qid, s, NEG_BIG)
            m_new = jnp.maximum(m, jnp.max(s, axis=0, keepdims=True))
            p = jnp.exp2(s - m_new)
            alpha = jnp.exp2(m - m_new)
            l = alpha * l + jnp.sum(p, axis=0, keepdims=True)
            v = v_all[h * HEAD_DIM:(h + 1) * HEAD_DIM, :]
            acc = alpha * acc + _dot(v, p.astype(BF16))
            out.append((m_new, l, acc))
        return tuple(out)

    init = tuple((jnp.full((1, tq), NEG_BIG, F32), jnp.zeros((1, tq), F32),
                  jnp.zeros((HEAD_DIM, tq), F32)) for _ in range(heads))
    carry = lax.fori_loop(0, qi, lambda kb, c: tile(kb, c, False), init)
    final = tile(qi, carry, True)
    for h in range(heads):
        _, l, acc = final[h]
        _head_norm_store(acc / l, g_ref, o_ref, h)


def _sb_kernel(q_ref, k_ref, v_ref, g_ref, o_ref, *, tq, heads):
    qi = pl.program_id(2)
    cb = SB_CHUNK
    nchunk = tq // cb
    r = lax.broadcasted_iota(I32, (cb + 16, 2 * cb), 0)
    c = lax.broadcasted_iota(I32, (cb + 16, 2 * cb), 1) % cb
    uu = jnp.where(((r < cb) & (c > r)) | (r == cb), 1.0, 0.0).astype(BF16)
    row = lax.broadcasted_iota(I32, (tq, tq), 0)
    col = lax.broadcasted_iota(I32, (tq, tq), 1)
    past = row < col

    def tile(kb, carry, masked):
        start = pl.multiple_of(kb * tq, tq)
        k_all = k_ref[0, pl.ds(start, tq), :]
        v_all = v_ref[0, kb]
        zs = [_dot(k_all[:, h * SLAB:(h + 1) * SLAB], q_ref[0, h * SLAB:(h + 1) * SLAB, :])
              for h in range(heads)]
        lbs, exts = [], []
        for h in range(heads):
            z = zs[h]
            sp = jnp.maximum(z, 0.0) + jnp.log2(1.0 + jnp.exp2(-jnp.abs(z)))
            lbs.append(z - sp)
            if masked:
                sp = jnp.where(past, sp, 0.0)
            hi, lo = _split2(sp)
            exts.append([_dot(uu, jnp.concatenate([hi[ci * cb:(ci + 1) * cb, :],
                                                   lo[ci * cb:(ci + 1) * cb, :]], axis=0))
                         for ci in range(nchunk)])
        out = []
        for h in range(heads):
            run, acc = carry[h]
            chunks = [None] * nchunk
            for ci in reversed(range(nchunk)):
                rows = slice(ci * cb, (ci + 1) * cb)
                a = jnp.exp2(lbs[h][rows, :] - exts[h][ci][:cb, :] - run)
                if masked:
                    a = jnp.where(past[rows, :], a, 0.0)
                chunks[ci] = a.astype(BF16)
                run = run + exts[h][ci][cb:cb + 1, :]
            v = v_all[h * HEAD_DIM:(h + 1) * HEAD_DIM, :]
            acc = acc + _dot(v, jnp.concatenate(chunks, axis=0))
            out.append((run, acc))
        return tuple(out)

    init = tuple((jnp.zeros((1, tq), F32), jnp.zeros((HEAD_DIM, tq), F32)) for _ in range(heads))
    carry = tile(qi, init, True)
    final = lax.fori_loop(0, qi, lambda i, cr: tile(qi - 1 - i, cr, False), carry)
    for h in range(heads):
        _head_norm_store(final[h][1], g_ref, o_ref, h)


def _attention(kernel, qt, k, vt, g_col, *, tq, heads, name):
    b, d_slab, s = qt.shape
    nk = vt.shape[1]
    d_val = N_HEADS * HEAD_DIM
    return pl.pallas_call(
        functools.partial(kernel, tq=tq, heads=heads),
        out_shape=jax.ShapeDtypeStruct((b, d_val, s), BF16),
        grid=(b, N_HEADS // heads, s // tq),
        in_specs=[pl.BlockSpec((1, heads * SLAB, tq), lambda i, h, j: (i, h, j)),
                  pl.BlockSpec((1, s, heads * SLAB), lambda i, h, j: (i, 0, h)),
                  pl.BlockSpec((1, nk, heads * HEAD_DIM, tq), lambda i, h, j: (i, 0, h, 0)),
                  pl.BlockSpec((heads * HEAD_DIM, 1), lambda i, h, j: (h, 0))],
        out_specs=pl.BlockSpec((1, heads * HEAD_DIM, tq), lambda i, h, j: (i, h, j)),
        compiler_params=_cparams(("arbitrary", "arbitrary", "arbitrary")),
        name=name,
    )(qt, k, vt, g_col)


def _outproj_kernel(of_ref, os_ref, x_ref, gate_ref, sc_ref, sh_ref, wa_ref, wb_ref,
                    g_ref, b_ref, wr_ref, x1_ref, up_ref, lg_ref, *, alpha):
    d = x_ref.shape[2]
    mix = _dot_tn(of_ref[0], wa_ref[...]) + _dot_tn(os_ref[0], wb_ref[...])
    x1 = _layer_norm(alpha * x_ref[0] + (1.0 + gate_ref[0]) * mix, g_ref[...], b_ref[...])
    x1_ref[...] = x1
    u2 = x1 * (1.0 + sc_ref[0]) + sh_ref[0]
    up_ref[...] = _pack(u2)
    lg_ref[...] = _dot3_nt(wr_ref[...], u2)


def _outproj(of_t, os_t, x, gate, sc, sh, wa, wb, g, bb, wr_t, *, t, alpha):
    b, s, d = x.shape
    n = b * s
    nt = s // t
    e = wr_t.shape[0]
    d_val = of_t.shape[1]
    mod_spec = pl.BlockSpec((1, 1, d), lambda i, j: (i, 0, 0))
    return pl.pallas_call(
        functools.partial(_outproj_kernel, alpha=alpha),
        out_shape=(jax.ShapeDtypeStruct((n, d), F32),
                   jax.ShapeDtypeStruct((n, d // 2), U32),
                   jax.ShapeDtypeStruct((e, n), F32)),
        grid=(b, nt),
        in_specs=[pl.BlockSpec((1, d_val, t), lambda i, j: (i, 0, j)),
                  pl.BlockSpec((1, d_val, t), lambda i, j: (i, 0, j)),
                  pl.BlockSpec((1, t, d), lambda i, j: (i, j, 0)),
                  mod_spec, mod_spec, mod_spec,
                  pl.BlockSpec(wa.shape, lambda i, j: (0, 0)),
                  pl.BlockSpec(wb.shape, lambda i, j: (0, 0)),
                  pl.BlockSpec((1, d), lambda i, j: (0, 0)),
                  pl.BlockSpec((1, d), lambda i, j: (0, 0)),
                  pl.BlockSpec(wr_t.shape, lambda i, j: (0, 0))],
        out_specs=(pl.BlockSpec((t, d), lambda i, j: (i * nt + j, 0)),
                   pl.BlockSpec((t, d // 2), lambda i, j: (i * nt + j, 0)),
                   pl.BlockSpec((e, t), lambda i, j: (0, i * nt + j))),
        compiler_params=_cparams(("arbitrary", "arbitrary")),
        name="outproj",
    )(of_t, os_t, x, gate, sc, sh, wa, wb, g, bb, wr_t)


def _route_kernel(lg_ref, bias_ref, idx_ref, rank_ref, w_ref, cnt_ref, *, t):
    @pl.when(pl.program_id(0) == 0)
    def _():
        cnt_ref[...] = jnp.zeros_like(cnt_ref)

    scores = jax.nn.sigmoid(lg_ref[...])
    sel = scores + bias_ref[...]
    neg_inf = -jnp.inf

    gscore = []
    for gi in range(N_GROUPS):
        blk = sel[gi * GROUP_SIZE:(gi + 1) * GROUP_SIZE, :]
        m1 = jnp.max(blk, axis=0, keepdims=True)
        is_max = blk == m1
        n_max = jnp.sum(jnp.where(is_max, 1.0, 0.0), axis=0, keepdims=True)
        m2 = jnp.max(jnp.where(is_max, neg_inf, blk), axis=0, keepdims=True)
        gscore.append(m1 + jnp.where(n_max >= 2.0, m1, m2))

    parts = []
    for gi in range(N_GROUPS):
        beaten = jnp.zeros_like(gscore[gi])
        for gj in range(N_GROUPS):
            if gj == gi:
                continue
            wins = (gscore[gj] > gscore[gi]) if gj > gi else (gscore[gj] >= gscore[gi])
            beaten = beaten + jnp.where(wins, 1.0, 0.0)
        keep = beaten < float(TOP_GROUPS)
        blk = sel[gi * GROUP_SIZE:(gi + 1) * GROUP_SIZE, :]
        parts.append(jnp.where(keep, blk, neg_inf))
    cand = jnp.concatenate(parts, axis=0)

    eidx = lax.broadcasted_iota(I32, cand.shape, 0).astype(F32)
    chosen = jnp.zeros(cand.shape, F32)
    idx_rows, w_rows = [], []
    for _ in range(TOP_K):
        mx = jnp.max(cand, axis=0, keepdims=True)
        idx = jnp.min(jnp.where(cand == mx, eidx, float(N_EXPERTS)), axis=0, keepdims=True)
        hit = eidx == idx
        w_rows.append(jnp.sum(jnp.where(hit, scores, 0.0), axis=0, keepdims=True))
        idx_rows.append(idx)
        cand = jnp.where(hit, neg_inf, cand)
        chosen = jnp.where(hit, 1.0, chosen)

    w_sum = w_rows[0]
    for wk in w_rows[1:]:
        w_sum = w_sum + wk

    r = lax.broadcasted_iota(I32, (t, t), 0)
    c = lax.broadcasted_iota(I32, (t, t), 1)
    before = jnp.where(r < c, 1.0, 0.0).astype(BF16)
    prefix = _dot(chosen.astype(BF16), before) + cnt_ref[...]
    for k in range(TOP_K):
        hit = eidx == idx_rows[k]
        rank = jnp.sum(jnp.where(hit, prefix, 0.0), axis=0, keepdims=True)
        idx_ref[k:k + 1, :] = idx_rows[k].astype(I32)
        rank_ref[k:k + 1, :] = rank.astype(I32)
        w_ref[k:k + 1, :] = w_rows[k] / w_sum * ROUTED_SCALE
    cnt_ref[...] = cnt_ref[...] + jnp.sum(chosen, axis=1, keepdims=True)


def _route(logits_t, bias_col, *, t):
    e, n = logits_t.shape
    row_spec = pl.BlockSpec((TOP_K, t), lambda i: (0, i))
    return pl.pallas_call(
        functools.partial(_route_kernel, t=t),
        out_shape=(jax.ShapeDtypeStruct((TOP_K, n), I32),
                   jax.ShapeDtypeStruct((TOP_K, n), I32),
                   jax.ShapeDtypeStruct((TOP_K, n), F32),
                   jax.ShapeDtypeStruct((e, 1), F32)),
        grid=(n // t,),
        in_specs=[pl.BlockSpec((e, t), lambda i: (0, i)),
                  pl.BlockSpec((e, 1), lambda i: (0, 0))],
        out_specs=(row_spec, row_spec, row_spec, pl.BlockSpec((e, 1), lambda i: (0, 0))),
        compiler_params=_cparams(("arbitrary",)),
        name="route",
    )(logits_t, bias_col)


def _dest_kernel(cnt_ref, idx_ref, rank_ref, dest_ref, bs_ref, nb_ref, pstart_ref):
    e = cnt_ref.shape[0]

    @pl.when(pl.program_id(0) == 0)
    def _():
        nblk = jnp.floor((cnt_ref[...] + float(MOE_BLOCK - 1)) * (1.0 / MOE_BLOCK))
        hi = jnp.floor(nblk * (1.0 / 32.0))
        lo = nblk - 32.0 * hi
        r = lax.broadcasted_iota(I32, (e, e), 0)
        c = lax.broadcasted_iota(I32, (e, e), 1)
        below = jnp.where(c < r, 1.0, 0.0).astype(BF16)
        hi_b = jnp.broadcast_to(hi, (e, SLAB)).astype(BF16)
        lo_b = jnp.broadcast_to(lo, (e, SLAB)).astype(BF16)
        bstart = 32.0 * _dot(below, hi_b) + _dot(below, lo_b)
        pstart_ref[...] = bstart[:, 0:1] * float(MOE_BLOCK)
        bs_ref[...] = bstart[:, 0:1].astype(I32)
        nb_ref[...] = nblk.astype(I32)

    eidx = lax.broadcasted_iota(I32, (e, idx_ref.shape[1]), 0)
    for k in range(TOP_K):
        hit = eidx == idx_ref[k:k + 1, :]
        base = jnp.sum(jnp.where(hit, pstart_ref[...], 0.0), axis=0, keepdims=True)
        dest_ref[k:k + 1, :] = base.astype(I32) + rank_ref[k:k + 1, :]


def _dest(counts, idx_t, rank_t, *, t):
    e = counts.shape[0]
    n = idx_t.shape[1]
    row_spec = pl.BlockSpec((TOP_K, t), lambda i: (0, i))
    return pl.pallas_call(
        _dest_kernel,
        out_shape=(jax.ShapeDtypeStruct((TOP_K, n), I32),
                   jax.ShapeDtypeStruct((e, 1), I32),
                   jax.ShapeDtypeStruct((e, 1), I32)),
        grid=(n // t,),
        in_specs=[pl.BlockSpec((e, 1), lambda i: (0, 0)), row_spec, row_spec],
        out_specs=(row_spec,
                   pl.BlockSpec((e, 1), lambda i: (0, 0)),
                   pl.BlockSpec((e, 1), lambda i: (0, 0))),
        scratch_shapes=[pltpu.VMEM((e, 1), F32)],
        compiler_params=_cparams(("arbitrary",)),
        name="dest",
    )(counts, idx_t, rank_t)


SC_CORES = 2
SC_SUBCORES = 16
SC_CHUNK = 64


def _sc_scatter_kernel(up_ref, dest_ref, xs_ref, rows, idx, sem, *, t, tiles_per_worker):
    wid = lax.axis_index("s") * SC_CORES + lax.axis_index("c")

    def body(i, _):
        tile = wid * tiles_per_worker + i
        pltpu.sync_copy(up_ref.at[pl.ds(pl.multiple_of(tile * t, t), t)], rows)
        pltpu.sync_copy(dest_ref.at[pl.ds(pl.multiple_of(tile * TOP_K, TOP_K), TOP_K)], idx)
        copies = [pltpu.async_copy(rows, xs_ref.at[idx.at[k]], sem) for k in range(TOP_K)]
        for cp in copies:
            cp.wait()
        return 0

    lax.fori_loop(0, tiles_per_worker, body, 0)


def _sc_scatter(up, dest_flat, cap, *, t):
    n, width = up.shape
    workers = SC_CORES * SC_SUBCORES
    tiles_per_worker = n // t // workers
    assert tiles_per_worker * workers * t == n
    mesh = plsc.VectorSubcoreMesh(core_axis_name="c", subcore_axis_name="s")
    return pl.kernel(
        functools.partial(_sc_scatter_kernel, t=t, tiles_per_worker=tiles_per_worker),
        out_type=jax.ShapeDtypeStruct((cap, width), up.dtype),
        mesh=mesh,
        scratch_types=[pltpu.VMEM((t, width), up.dtype), pltpu.VMEM((TOP_K, t), I32), pltpu.SemaphoreType.DMA],
        name="sc_scatter",
    )(up, dest_flat.reshape(-1, t))


def _unpack(words):
    lo, hi = _unpack_f32(words)
    return lo.astype(BF16), hi.astype(BF16)


def _expert_kernel(bs_ref, nb_ref, cnt_ref, xs_ref, wg_ref, wu_ref, wd_ref, ys_ref,
                   wgu_s, wd_s, xbuf, ybuf, xsem, ysem, *, n_blocks, n_experts):
    e = pl.program_id(0)
    last = n_experts - 1
    d_half = xbuf.shape[2]
    d_exp = wg_ref.shape[2]
    nb = nb_ref[e]
    first = bs_ref[e]
    used = bs_ref[last] + nb_ref[last]

    def rows_of(blk):
        return pl.ds(pl.multiple_of(blk * MOE_BLOCK, MOE_BLOCK), MOE_BLOCK)

    def x_copy(blk, slot):
        return pltpu.make_async_copy(xs_ref.at[rows_of(blk)], xbuf.at[slot], xsem.at[slot])

    def y_copy(blk, slot):
        return pltpu.make_async_copy(ybuf.at[slot], ys_ref.at[rows_of(blk)], ysem.at[slot])

    @pl.when((e == 0) & (used > 0))
    def _():
        x_copy(0, 0).start()

    @pl.when(nb > 0)
    def _():
        wgu_s[:, :d_exp] = wg_ref[0].astype(BF16)
        wgu_s[:, d_exp:] = wu_ref[0].astype(BF16)
        wd_s[...] = wd_ref[0].astype(BF16)

        def block(g, _):
            slot = lax.rem(g, 2)
            x_copy(g, slot).wait()

            @pl.when(g + 1 < used)
            def _():
                x_copy(g + 1, 1 - slot).start()

            @pl.when(g >= 2)
            def _():
                y_copy(g - 2, slot).wait()

            rows = MOE_BLOCK // MOE_SPLIT
            for part in range(MOE_SPLIT):
                sl = slice(part * rows, (part + 1) * rows)
                row = lax.broadcasted_iota(I32, (rows, d_half), 0) + (part * rows)
                live = row < cnt_ref[e] - (g - first) * MOE_BLOCK
                lo, hi = _unpack(jnp.where(live, xbuf[slot, sl, :], jnp.uint32(0)))
                gu = _dot(lo, wgu_s[:d_half, :]) + _dot(hi, wgu_s[d_half:, :])
                gate = gu[:, :d_exp]
                h = gate * jax.nn.sigmoid(gate) * gu[:, d_exp:]
                ybuf[slot, sl, :] = _pack(_dot(h.astype(BF16), wd_s[...]))
            y_copy(g, slot).start()
            return 0

        lax.fori_loop(first, first + nb, block, 0)

    @pl.when(e == last)
    def _():
        @pl.when(used >= 2)
        def _():
            y_copy(used - 2, lax.rem(used, 2)).wait()

        @pl.when(used >= 1)
        def _():
            y_copy(used - 1, lax.rem(used - 1, 2)).wait()

        ybuf[0] = jnp.zeros(ybuf.shape[1:], ybuf.dtype)

        def fill(blk, _):
            y_copy(blk, 0).start()
            return 0

        def done(blk, _):
            y_copy(blk, 0).wait()
            return 0

        lax.fori_loop(used, n_blocks, fill, 0)
        lax.fori_loop(used, n_blocks, done, 0)


def _experts(blk_start, blk_count, counts, xs, wg, wu, wd):
    cap, d_half = xs.shape
    d = 2 * d_half
    n_exp, _, d_exp = wg.shape
    grid_spec = pltpu.PrefetchScalarGridSpec(
        num_scalar_prefetch=3,
        grid=(n_exp,),
        in_specs=[pl.BlockSpec(memory_space=pl.ANY),
                  pl.BlockSpec((1, d, d_exp), lambda i, bs, nb, cnt: (i, 0, 0)),
                  pl.BlockSpec((1, d, d_exp), lambda i, bs, nb, cnt: (i, 0, 0)),
                  pl.BlockSpec((1, d_exp, d), lambda i, bs, nb, cnt: (i, 0, 0))],
        out_specs=pl.BlockSpec(memory_space=pl.ANY),
        scratch_shapes=[pltpu.VMEM((d, 2 * d_exp), BF16), pltpu.VMEM((d_exp, d), BF16),
                        pltpu.VMEM((2, MOE_BLOCK, d_half), U32), pltpu.VMEM((2, MOE_BLOCK, d_half), U32),
                        pltpu.SemaphoreType.DMA((2,)), pltpu.SemaphoreType.DMA((2,))],
    )
    return pl.pallas_call(
        functools.partial(_expert_kernel, n_blocks=cap // MOE_BLOCK, n_experts=n_exp),
        out_shape=jax.ShapeDtypeStruct((cap, d_half), U32),
        grid_spec=grid_spec,
        compiler_params=_cparams(("arbitrary",)),
        name="experts",
    )(blk_start, blk_count, counts, xs, wg, wu, wd)


def _sc_gather_kernel(table_ref, idx_ref, out_ref, idx, rows0, rows1, sem0, sem1, *, per_worker):
    wid = lax.axis_index("s") * SC_CORES + lax.axis_index("c")
    base = wid * per_worker
    chunks = per_worker // SC_CHUNK
    pltpu.sync_copy(idx_ref.at[pl.ds(pl.multiple_of(wid * chunks, chunks), chunks)], idx)

    def body(i, _):
        off0 = pl.multiple_of(base + (2 * i) * SC_CHUNK, SC_CHUNK)
        off1 = pl.multiple_of(off0 + SC_CHUNK, SC_CHUNK)
        g0 = pltpu.async_copy(table_ref.at[idx.at[2 * i]], rows0, sem0)
        g1 = pltpu.async_copy(table_ref.at[idx.at[2 * i + 1]], rows1, sem1)
        g0.wait()
        pltpu.sync_copy(rows0, out_ref.at[pl.ds(off0, SC_CHUNK)])
        g1.wait()
        pltpu.sync_copy(rows1, out_ref.at[pl.ds(off1, SC_CHUNK)])
        return 0

    lax.fori_loop(0, per_worker // (2 * SC_CHUNK), body, 0)


def _sc_gather(table, idx):
    rows = idx.shape[0]
    width = table.shape[1]
    workers = SC_CORES * SC_SUBCORES
    per_worker = rows // workers
    assert per_worker * workers == rows and per_worker % (2 * SC_CHUNK) == 0
    mesh = plsc.VectorSubcoreMesh(core_axis_name="c", subcore_axis_name="s")
    return pl.kernel(
        functools.partial(_sc_gather_kernel, per_worker=per_worker),
        out_type=jax.ShapeDtypeStruct((rows, width), table.dtype),
        mesh=mesh,
        scratch_types=[pltpu.VMEM((per_worker // SC_CHUNK, SC_CHUNK), I32),
                       pltpu.VMEM((SC_CHUNK, width), table.dtype), pltpu.VMEM((SC_CHUNK, width), table.dtype),
                       pltpu.SemaphoreType.DMA, pltpu.SemaphoreType.DMA],
        name="sc_gather",
    )(table, idx.reshape(-1, SC_CHUNK))


def _combine_kernel(yg_ref, w_ref, x1_ref, gate_ref, sc_ref, sh_ref,
                    wgu_ref, wd_ref, g_ref, b_ref, o_ref, *, alpha):
    x1 = x1_ref[...]
    ub = (x1 * (1.0 + sc_ref[0]) + sh_ref[0]).astype(BF16)
    d_sh = wd_ref.shape[0]
    gu = _dot(ub, wgu_ref[...])
    gate = gu[:, :d_sh]
    h = gate * jax.nn.sigmoid(gate) * gu[:, d_sh:]
    ffn = _dot(h.astype(BF16), wd_ref[...])

    w = w_ref[...]
    d_half = yg_ref.shape[3]
    lo_sum = ffn[:, :d_half]
    hi_sum = ffn[:, d_half:]
    for k in range(TOP_K):
        lo, hi = _unpack_f32(yg_ref[0, k])
        lo_sum = lo_sum + w[:, k:k + 1] * lo
        hi_sum = hi_sum + w[:, k:k + 1] * hi
    ffn = jnp.concatenate([lo_sum, hi_sum], axis=1)
    o_ref[...] = _layer_norm(alpha * x1 + (1.0 + gate_ref[0]) * ffn, g_ref[...], b_ref[...])


def _combine(yg, w_tok, x1, gate, sc, sh, wgu, wd, g, bb, *, t, seq, alpha):
    n, d = x1.shape
    per_batch = seq // t
    mod_spec = pl.BlockSpec((1, 1, d), lambda i: (i // per_batch, 0, 0))
    return pl.pallas_call(
        functools.partial(_combine_kernel, alpha=alpha),
        out_shape=jax.ShapeDtypeStruct((n, d), F32),
        grid=(n // t,),
        in_specs=[pl.BlockSpec((1, TOP_K, t, d // 2), lambda i: (i, 0, 0, 0)),
                  pl.BlockSpec((t, TOP_K), lambda i: (i, 0)),
                  pl.BlockSpec((t, d), lambda i: (i, 0)),
                  mod_spec, mod_spec, mod_spec,
                  pl.BlockSpec(wgu.shape, lambda i: (0, 0)),
                  pl.BlockSpec(wd.shape, lambda i: (0, 0)),
                  pl.BlockSpec((1, d), lambda i: (0, 0)),
                  pl.BlockSpec((1, d), lambda i: (0, 0))],
        out_specs=pl.BlockSpec((t, d), lambda i: (i, 0)),
        compiler_params=_cparams(("arbitrary",)),
        name="combine",
    )(yg, w_tok, x1, gate, sc, sh, wgu, wd, g, bb)


def _slab_cols(w, scale=1.0):
    d = w.shape[0]
    w = (w * scale).reshape(d, N_HEADS, HEAD_DIM)
    w = jnp.pad(w, ((0, 0), (0, 0), (0, SLAB - HEAD_DIM)))
    return w.reshape(d, N_HEADS * SLAB)


def _placement():
    h = jnp.arange(SLAB)[:, None]
    c = jnp.arange(N_HEADS * SLAB)[None, :]
    mats = [((h < N_HEADS) & (c == h * SLAB + HEAD_DIM + p)) for p in range(3)]
    return jnp.stack(mats).astype(BF16)


def _tile_major(rows, t):
    k, n = rows.shape
    return rows.reshape(k, n // t, t).transpose(1, 0, 2).reshape(-1)


def _layer(x, c_pad, w_ada, b_ada, w_in, b_f, fox_g, sb_g, w_out, ln1_g, ln1_b,
           w_router, router_bias, w_gate_e, w_up_e, w_down_e,
           w_gate_sh, w_up_sh, w_down_sh, ln2_g, ln2_b, *, alpha,
           t_proj=512, t_attn=512, attn_heads=4, t_route=512, t_moe=128):
    b, s, d = x.shape
    n = b * s
    d_val = N_HEADS * HEAD_DIM
    scale = HEAD_DIM ** -0.5 * LOG2_E

    ada = _ada(c_pad, w_ada, b_ada.reshape(1, -1))[:b]
    shift1, scale1, gate1, shift2, scale2, gate2 = [m[:, None, :] for m in jnp.split(ada, 6, axis=-1)]

    q_f, k_f, v_f, q_s, k_s, v_s, w_f = jnp.split(
        w_in, [d_val, 2 * d_val, 3 * d_val, 4 * d_val, 5 * d_val, 6 * d_val], axis=1)
    w_f = jnp.pad(w_f, ((0, 0), (0, SLAB - N_HEADS)))
    wtok = jnp.concatenate([_slab_cols(k_f), _slab_cols(k_s), w_f], axis=1).astype(BF16)
    wfeat = jnp.concatenate([_slab_cols(q_f, scale), _slab_cols(q_s, scale), v_f, v_s], axis=1).T.astype(BF16)
    bf_row = jnp.pad(b_f, (0, SLAB - N_HEADS)).reshape(1, SLAB)

    kf, ks, qf_t, qs_t, vf_t, vs_t = _inproj(x, scale1, shift1, wtok, wfeat, bf_row, _placement(),
                                             t=t_proj, tk=t_attn)
    of_t = _attention(_fox_kernel, qf_t, kf, vf_t, fox_g.reshape(-1, 1),
                      tq=t_attn, heads=attn_heads, name="fox")
    os_t = _attention(_sb_kernel, qs_t, ks, vs_t, sb_g.reshape(-1, 1),
                      tq=t_attn, heads=attn_heads, name="sb")

    x1, u_packed, logits_t = _outproj(
        of_t, os_t, x, gate1, scale2, shift2,
        w_out[:d_val].astype(BF16), w_out[d_val:].astype(BF16),
        ln1_g.reshape(1, d), ln1_b.reshape(1, d), w_router.T, t=t_proj, alpha=alpha)

    idx_t, rank_t, w_t, counts = _route(logits_t, router_bias.reshape(-1, 1), t=t_route)
    n_blocks = -(-(n * TOP_K) // MOE_BLOCK) + N_EXPERTS
    dest_t, blk_start, blk_count = _dest(counts, idx_t, rank_t, t=t_route)
    dest_flat = _tile_major(dest_t, t_moe)

    xs = _sc_scatter(u_packed, dest_flat, n_blocks * MOE_BLOCK, t=t_moe)
    ys = _experts(blk_start.reshape(-1), blk_count.reshape(-1), counts.astype(I32).reshape(-1),
                  xs, w_gate_e, w_up_e, w_down_e)

    wgu_sh = jnp.concatenate([w_gate_sh, w_up_sh], axis=1).astype(BF16)
    yg = _sc_gather(ys, dest_flat).reshape(n // t_moe, TOP_K, t_moe, d // 2)
    out = _combine(yg, w_t.T, x1, gate2, scale2, shift2, wgu_sh, w_down_sh.astype(BF16),
                   ln2_g.reshape(1, d), ln2_b.reshape(1, d), t=t_moe, seq=s, alpha=alpha)
    return out.reshape(b, s, d)


def kernel(x, c, w_ada, b_ada, w_in, b_f, fox_norm_g, sb_norm_g, w_out, ln1_g, ln1_b, w_router, router_bias, w_gate_e, w_up_e, w_down_e, w_gate_sh, w_up_sh, w_down_sh, ln2_g, ln2_b):
    depth = w_ada.shape[0]
    alpha = (2.0 * depth) ** 0.25
    c_pad = jnp.pad(c, ((0, (-c.shape[0]) % 8), (0, 0)))
    for l in range(depth):
        x = _layer(x, c_pad, w_ada[l], b_ada[l], w_in[l], b_f[l], fox_norm_g[l], sb_norm_g[l],
                   w_out[l], ln1_g[l], ln1_b[l], w_router[l], router_bias[l],
                   w_gate_e[l], w_up_e[l], w_down_e[l], w_gate_sh[l], w_up_sh[l], w_down_sh[l],
                   ln2_g[l], ln2_b[l], alpha=alpha)
    return x
```

```python
import functools

import jax
import jax.numpy as jnp
from jax import lax
from jax.experimental import pallas as pl
from jax.experimental.pallas import tpu as pltpu
from jax.experimental.pallas import tpu_sc as plsc

F32 = jnp.float32
BF16 = jnp.bfloat16
I32 = jnp.int32
U32 = jnp.uint32

HEAD_DIM = 64
N_HEADS = 8
SLAB = 128
N_EXPERTS = 256
TOP_K = 8
N_GROUPS = 8
TOP_GROUPS = 4
GROUP_SIZE = N_EXPERTS // N_GROUPS
ROUTED_SCALE = 2.5
MOE_BLOCK = 256
MOE_SPLIT = 2
SB_CHUNK = 128
LN_EPS = 1e-5
RMS_EPS = 1e-6
NEG_BIG = -1e30
LOG2_E = 1.4426950408889634

VMEM_LIMIT = 56 * 1024 * 1024


def _cparams(sem):
    return pltpu.CompilerParams(dimension_semantics=sem, vmem_limit_bytes=VMEM_LIMIT)


def _dot(a, b):
    return jnp.dot(a, b, preferred_element_type=F32)


def _dot_nt(a, b):
    return lax.dot_general(a, b, (((1,), (1,)), ((), ())), preferred_element_type=F32)


def _dot_tn(a, b):
    return lax.dot_general(a, b, (((0,), (0,)), ((), ())), preferred_element_type=F32)


def _split2(x):
    hi = x.astype(BF16)
    lo = (x - hi.astype(F32)).astype(BF16)
    return hi, lo


def _split3(x):
    p1 = x.astype(BF16)
    r1 = x - p1.astype(F32)
    p2 = r1.astype(BF16)
    p3 = (r1 - p2.astype(F32)).astype(BF16)
    return p1, p2, p3


def _dot3(a, b):
    a_hi, a_lo = _split2(a)
    b_hi, b_lo = _split2(b)
    return _dot(a_hi, b_hi) + _dot(a_lo, b_hi) + _dot(a_hi, b_lo)


def _dot3_nt(a, b):
    a_hi, a_lo = _split2(a)
    b_hi, b_lo = _split2(b)
    return _dot_nt(a_hi, b_hi) + _dot_nt(a_lo, b_hi) + _dot_nt(a_hi, b_lo)


def _pack(x):
    half = x.shape[1] // 2
    lo = lax.bitcast_convert_type(x[:, :half].astype(BF16).astype(F32), U32)
    hi = lax.bitcast_convert_type(x[:, half:].astype(BF16).astype(F32), U32)
    return (lo >> 16) | (hi & jnp.uint32(0xFFFF0000))


def _unpack_f32(words):
    lo = lax.bitcast_convert_type(words << 16, F32)
    hi = lax.bitcast_convert_type(words & jnp.uint32(0xFFFF0000), F32)
    return lo, hi


def _softplus(z):
    return jnp.maximum(z, 0.0) + jnp.log(1.0 + jnp.exp(-jnp.abs(z)))


def _layer_norm(v, g, b):
    mu = jnp.mean(v, axis=-1, keepdims=True)
    d = v - mu
    var = jnp.mean(d * d, axis=-1, keepdims=True)
    return d * lax.rsqrt(var + LN_EPS) * g + b


def _ada_kernel(c_ref, w_ref, b_ref, o_ref):
    c = c_ref[...]
    s = c * jax.nn.sigmoid(c)
    o_ref[...] = _dot3(s, w_ref[...]) + b_ref[...]


def _ada(c_pad, w, b):
    rows, d = c_pad.shape
    n = w.shape[1]
    tn = 1024
    return pl.pallas_call(
        _ada_kernel,
        out_shape=jax.ShapeDtypeStruct((rows, n), F32),
        grid=(n // tn,),
        in_specs=[pl.BlockSpec((rows, d), lambda j: (0, 0)),
                  pl.BlockSpec((d, tn), lambda j: (0, j)),
                  pl.BlockSpec((1, tn), lambda j: (0, j))],
        out_specs=pl.BlockSpec((rows, tn), lambda j: (0, j)),
        compiler_params=_cparams(("arbitrary",)),
        name="ada",
    )(c_pad, w, b)


def _inproj_kernel(x_ref, sc_ref, sh_ref, wtok_ref, wfeat_ref, bf_ref, place_ref,
                   kf_ref, ks_ref, qf_ref, qs_ref, vf_ref, vs_ref, carry_ref, *, t, tk):
    d_slab = N_HEADS * SLAB
    d_val = N_HEADS * HEAD_DIM

    @pl.when(pl.program_id(1) == 0)
    def _():
        carry_ref[...] = jnp.zeros_like(carry_ref)

    u = x_ref[0] * (1.0 + sc_ref[0]) + sh_ref[0]
    ub = u.astype(BF16)

    flog = _dot(ub, wtok_ref[:, 2 * d_slab:]) + bf_ref[...]
    logf = -_softplus(-flog)
    lane = lax.broadcasted_iota(I32, logf.shape, 1)
    logf = jnp.where(lane < N_HEADS, logf, 0.0)
    row = lax.broadcasted_iota(I32, (t, t), 0)
    col = lax.broadcasted_iota(I32, (t, t), 1)
    tri = jnp.where(row >= col, 1.0, 0.0).astype(BF16)
    lf_hi, lf_lo = _split2(logf)
    cum = _dot(tri, lf_hi) + _dot(tri, lf_lo) + carry_ref[...]
    carry_ref[...] = cum[t - 1:t, :]
    n1, n2, n3 = _split3(-LOG2_E * cum)
    extras = _dot(n1, place_ref[0]) + _dot(n2, place_ref[1]) + _dot(n3, place_ref[2])

    kf_ref[0] = (_dot(ub, wtok_ref[:, :d_slab]) + extras).astype(BF16)
    ks_ref[0] = _dot(ub, wtok_ref[:, d_slab:2 * d_slab]).astype(BF16)

    qf = _dot_nt(wfeat_ref[:d_slab, :], ub)
    r = lax.broadcasted_iota(I32, qf.shape, 0) % SLAB
    qf = jnp.where((r >= HEAD_DIM) & (r < HEAD_DIM + 3), 1.0, qf)
    qf_ref[0] = qf.astype(BF16)
    qs_ref[0] = _dot_nt(wfeat_ref[d_slab:2 * d_slab, :], ub).astype(BF16)
    vf = _dot_nt(wfeat_ref[2 * d_slab:2 * d_slab + d_val, :], ub).astype(BF16)
    vs = _dot_nt(wfeat_ref[2 * d_slab + d_val:, :], ub).astype(BF16)
    for c in range(t // tk):
        vf_ref[0, c] = vf[:, c * tk:(c + 1) * tk]
        vs_ref[0, c] = vs[:, c * tk:(c + 1) * tk]


def _inproj(x, sc, sh, wtok, wfeat, bf_row, place, *, t, tk):
    b, s, d = x.shape
    d_slab = N_HEADS * SLAB
    d_val = N_HEADS * HEAD_DIM
    nt = s // t
    out_shape = (
        jax.ShapeDtypeStruct((b, s, d_slab), BF16),
        jax.ShapeDtypeStruct((b, s, d_slab), BF16),
        jax.ShapeDtypeStruct((b, d_slab, s), BF16),
        jax.ShapeDtypeStruct((b, d_slab, s), BF16),
        jax.ShapeDtypeStruct((b, s // tk, d_val, tk), BF16),
        jax.ShapeDtypeStruct((b, s // tk, d_val, tk), BF16),
    )
    tok_spec = pl.BlockSpec((1, t, d_slab), lambda i, j: (i, j, 0))
    feat_spec = pl.BlockSpec((1, d_slab, t), lambda i, j: (i, 0, j))
    val_spec = pl.BlockSpec((1, t // tk, d_val, tk), lambda i, j: (i, j, 0, 0))
    return pl.pallas_call(
        functools.partial(_inproj_kernel, t=t, tk=tk),
        out_shape=out_shape,
        grid=(b, nt),
        in_specs=[pl.BlockSpec((1, t, d), lambda i, j: (i, j, 0)),
                  pl.BlockSpec((1, 1, d), lambda i, j: (i, 0, 0)),
                  pl.BlockSpec((1, 1, d), lambda i, j: (i, 0, 0)),
                  pl.BlockSpec(wtok.shape, lambda i, j: (0, 0)),
                  pl.BlockSpec(wfeat.shape, lambda i, j: (0, 0)),
                  pl.BlockSpec((1, SLAB), lambda i, j: (0, 0)),
                  pl.BlockSpec(place.shape, lambda i, j: (0, 0, 0))],
        out_specs=(tok_spec, tok_spec, feat_spec, feat_spec, val_spec, val_spec),
        scratch_shapes=[pltpu.VMEM((1, SLAB), F32)],
        compiler_params=_cparams(("arbitrary", "arbitrary")),
        name="inproj",
    )(x, sc, sh, wtok, wfeat, bf_row, place)


def _head_norm_store(acc, g_ref, o_ref, h):
    rows = slice(h * HEAD_DIM, (h + 1) * HEAD_DIM)
    ms = jnp.mean(acc * acc, axis=0, keepdims=True)
    o_ref[0, rows, :] = (acc * lax.rsqrt(ms + RMS_EPS) * g_ref[rows, :]).astype(BF16)


def _fox_kernel(q_ref, k_ref, v_ref, g_ref, o_ref, *, tq, heads):
    qi = pl.program_id(2)

    def tile(kb, carry, masked):
        start = pl.multiple_of(kb * tq, tq)
        k_all = k_ref[0, pl.ds(start, tq), :]
        v_all = v_ref[0, kb]
        scores = [_dot(k_all[:, h * SLAB:(h + 1) * SLAB], q_ref[0, h * SLAB:(h + 1) * SLAB, :])
                  for h in range(heads)]
        out = []
        for h in range(heads):
            m, l, acc = carry[h]
            s = scores[h]
            if masked:
                kid = lax.broadcasted_iota(I32, s.shape, 0)
                qid = lax.broadcasted_iota(I32, s.shape, 1)
                s = jnp.where(kid <= qid, s, NEG_BIG)
            m_new = jnp.maximum(m, jnp.max(s, axis=0, keepdims=True))
            p = jnp.exp2(s - m_new)
            alpha = jnp.exp2(m - m_new)
            l = alpha * l + jnp.sum(p, axis=0, keepdims=True)
            v = v_all[h * HEAD_DIM:(h + 1) * HEAD_DIM, :]
            acc = alpha * acc + _dot(v, p.astype(BF16))
            out.append((m_new, l, acc))
        return tuple(out)

    init = tuple((jnp.full((1, tq), NEG_BIG, F32), jnp.zeros((1, tq), F32),
                  jnp.zeros((HEAD_DIM, tq), F32)) for _ in range(heads))
    carry = lax.fori_loop(0, qi, lambda kb, c: tile(kb, c, False), init)
    final = tile(qi, carry, True)
    for h in range(heads):
        _, l, acc = final[h]
        _head_norm_store(acc / l, g_ref, o_ref, h)


def _sb_kernel(q_ref, k_ref, v_ref, g_ref, o_ref, *, tq, heads):
    qi = pl.program_id(2)
    cb = SB_CHUNK
    nchunk = tq // cb
    r = lax.broadcasted_iota(I32, (cb + 16, 2 * cb), 0)
    c = lax.broadcasted_iota(I32, (cb + 16, 2 * cb), 1) % cb
    uu = jnp.where(((r < cb) & (c > r)) | (r == cb), 1.0, 0.0).astype(BF16)
    row = lax.broadcasted_iota(I32, (tq, tq), 0)
    col = lax.broadcasted_iota(I32, (tq, tq), 1)
    past = row < col

    def tile(kb, carry, masked):
        start = pl.multiple_of(kb * tq, tq)
        k_all = k_ref[0, pl.ds(start, tq), :]
        v_all = v_ref[0, kb]
        zs = [_dot(k_all[:, h * SLAB:(h + 1) * SLAB], q_ref[0, h * SLAB:(h + 1) * SLAB, :])
              for h in range(heads)]
        lbs, exts = [], []
        for h in range(heads):
            z = zs[h]
            sp = jnp.maximum(z, 0.0) + jnp.log2(1.0 + jnp.exp2(-jnp.abs(z)))
            lbs.append(z - sp)
            if masked:
                sp = jnp.where(past, sp, 0.0)
            hi, lo = _split2(sp)
            exts.append([_dot(uu, jnp.concatenate([hi[ci * cb:(ci + 1) * cb, :],
                                                   lo[ci * cb:(ci + 1) * cb, :]], axis=0))
                         for ci in range(nchunk)])
        out = []
        for h in range(heads):
            run, acc = carry[h]
            chunks = [None] * nchunk
            for ci in reversed(range(nchunk)):
                rows = slice(ci * cb, (ci + 1) * cb)
                a = jnp.exp2(lbs[h][rows, :] - exts[h][ci][:cb, :] - run)
                if masked:
                    a = jnp.where(past[rows, :], a, 0.0)
                chunks[ci] = a.astype(BF16)
                run = run + exts[h][ci][cb:cb + 1, :]
            v = v_all[h * HEAD_DIM:(h + 1) * HEAD_DIM, :]
            acc = acc + _dot(v, jnp.concatenate(chunks, axis=0))
            out.append((run, acc))
        return tuple(out)

    init = tuple((jnp.zeros((1, tq), F32), jnp.zeros((HEAD_DIM, tq), F32)) for _ in range(heads))
    carry = tile(qi, init, True)
    final = lax.fori_loop(0, qi, lambda i, cr: tile(qi - 1 - i, cr, False), carry)
    for h in range(heads):
        _head_norm_store(final[h][1], g_ref, o_ref, h)


def _attention(kernel, qt, k, vt, g_col, *, tq, heads, name):
    b, d_slab, s = qt.shape
    nk = vt.shape[1]
    d_val = N_HEADS * HEAD_DIM
    return pl.pallas_call(
        functools.partial(kernel, tq=tq, heads=heads),
        out_shape=jax.ShapeDtypeStruct((b, d_val, s), BF16),
        grid=(b, N_HEADS // heads, s // tq),
        in_specs=[pl.BlockSpec((1, heads * SLAB, tq), lambda i, h, j: (i, h, j)),
                  pl.BlockSpec((1, s, heads * SLAB), lambda i, h, j: (i, 0, h)),
                  pl.BlockSpec((1, nk, heads * HEAD_DIM, tq), lambda i, h, j: (i, 0, h, 0)),
                  pl.BlockSpec((heads * HEAD_DIM, 1), lambda i, h, j: (h, 0))],
        out_specs=pl.BlockSpec((1, heads * HEAD_DIM, tq), lambda i, h, j: (i, h, j)),
        compiler_params=_cparams(("arbitrary", "arbitrary", "arbitrary")),
        name=name,
    )(qt, k, vt, g_col)


def _outproj_kernel(of_ref, os_ref, x_ref, gate_ref, sc_ref, sh_ref, wa_ref, wb_ref,
                    g_ref, b_ref, wr_ref, x1_ref, up_ref, lg_ref, *, alpha):
    d = x_ref.shape[2]
    mix = _dot_tn(of_ref[0], wa_ref[...]) + _dot_tn(os_ref[0], wb_ref[...])
    x1 = _layer_norm(alpha * x_ref[0] + (1.0 + gate_ref[0]) * mix, g_ref[...], b_ref[...])
    x1_ref[...] = x1
    u2 = x1 * (1.0 + sc_ref[0]) + sh_ref[0]
    up_ref[...] = _pack(u2)
    lg_ref[...] = _dot3_nt(wr_ref[...], u2)


def _outproj(of_t, os_t, x, gate, sc, sh, wa, wb, g, bb, wr_t, *, t, alpha):
    b, s, d = x.shape
    n = b * s
    nt = s // t
    e = wr_t.shape[0]
    d_val = of_t.shape[1]
    mod_spec = pl.BlockSpec((1, 1, d), lambda i, j: (i, 0, 0))
    return pl.pallas_call(
        functools.partial(_outproj_kernel, alpha=alpha),
        out_shape=(jax.ShapeDtypeStruct((n, d), F32),
                   jax.ShapeDtypeStruct((n, d // 2), U32),
                   jax.ShapeDtypeStruct((e, n), F32)),
        grid=(b, nt),
        in_specs=[pl.BlockSpec((1, d_val, t), lambda i, j: (i, 0, j)),
                  pl.BlockSpec((1, d_val, t), lambda i, j: (i, 0, j)),
                  pl.BlockSpec((1, t, d), lambda i, j: (i, j, 0)),
                  mod_spec, mod_spec, mod_spec,
                  pl.BlockSpec(wa.shape, lambda i, j: (0, 0)),
                  pl.BlockSpec(wb.shape, lambda i, j: (0, 0)),
                  pl.BlockSpec((1, d), lambda i, j: (0, 0)),
                  pl.BlockSpec((1, d), lambda i, j: (0, 0)),
                  pl.BlockSpec(wr_t.shape, lambda i, j: (0, 0))],
        out_specs=(pl.BlockSpec((t, d), lambda i, j: (i * nt + j, 0)),
                   pl.BlockSpec((t, d // 2), lambda i, j: (i * nt + j, 0)),
                   pl.BlockSpec((e, t), lambda i, j: (0, i * nt + j))),
        compiler_params=_cparams(("arbitrary", "arbitrary")),
        name="outproj",
    )(of_t, os_t, x, gate, sc, sh, wa, wb, g, bb, wr_t)


def _route_kernel(lg_ref, bias_ref, idx_ref, rank_ref, w_ref, cnt_ref, *, t):
    @pl.when(pl.program_id(0) == 0)
    def _():
        cnt_ref[...] = jnp.zeros_like(cnt_ref)

    scores = jax.nn.sigmoid(lg_ref[...])
    sel = scores + bias_ref[...]
    neg_inf = -jnp.inf

    gscore = []
    for gi in range(N_GROUPS):
        blk = sel[gi * GROUP_SIZE:(gi + 1) * GROUP_SIZE, :]
        m1 = jnp.max(blk, axis=0, keepdims=True)
        is_max = blk == m1
        n_max = jnp.sum(jnp.where(is_max, 1.0, 0.0), axis=0, keepdims=True)
        m2 = jnp.max(jnp.where(is_max, neg_inf, blk), axis=0, keepdims=True)
        gscore.append(m1 + jnp.where(n_max >= 2.0, m1, m2))

    parts = []
    for gi in range(N_GROUPS):
        beaten = jnp.zeros_like(gscore[gi])
        for gj in range(N_GROUPS):
            if gj == gi:
                continue
            wins = (gscore[gj] > gscore[gi]) if gj > gi else (gscore[gj] >= gscore[gi])
            beaten = beaten + jnp.where(wins, 1.0, 0.0)
        keep = beaten < float(TOP_GROUPS)
        blk = sel[gi * GROUP_SIZE:(gi + 1) * GROUP_SIZE, :]
        parts.append(jnp.where(keep, blk, neg_inf))
    cand = jnp.concatenate(parts, axis=0)

    eidx = lax.broadcasted_iota(I32, cand.shape, 0).astype(F32)
    chosen = jnp.zeros(cand.shape, F32)
    idx_rows, w_rows = [], []
    for _ in range(TOP_K):
        mx = jnp.max(cand, axis=0, keepdims=True)
        idx = jnp.min(jnp.where(cand == mx, eidx, float(N_EXPERTS)), axis=0, keepdims=True)
        hit = eidx == idx
        w_rows.append(jnp.sum(jnp.where(hit, scores, 0.0), axis=0, keepdims=True))
        idx_rows.append(idx)
        cand = jnp.where(hit, neg_inf, cand)
        chosen = jnp.where(hit, 1.0, chosen)

    w_sum = w_rows[0]
    for wk in w_rows[1:]:
        w_sum = w_sum + wk

    r = lax.broadcasted_iota(I32, (t, t), 0)
    c = lax.broadcasted_iota(I32, (t, t), 1)
    before = jnp.where(r < c, 1.0, 0.0).astype(BF16)
    prefix = _dot(chosen.astype(BF16), before) + cnt_ref[...]
    for k in range(TOP_K):
        hit = eidx == idx_rows[k]
        rank = jnp.sum(jnp.where(hit, prefix, 0.0), axis=0, keepdims=True)
        idx_ref[k:k + 1, :] = idx_rows[k].astype(I32)
        rank_ref[k:k + 1, :] = rank.astype(I32)
        w_ref[k:k + 1, :] = w_rows[k] / w_sum * ROUTED_SCALE
    cnt_ref[...] = cnt_ref[...] + jnp.sum(chosen, axis=1, keepdims=True)


def _route(logits_t, bias_col, *, t):
    e, n = logits_t.shape
    row_spec = pl.BlockSpec((TOP_K, t), lambda i: (0, i))
    return pl.pallas_call(
        functools.partial(_route_kernel, t=t),
        out_shape=(jax.ShapeDtypeStruct((TOP_K, n), I32),
                   jax.ShapeDtypeStruct((TOP_K, n), I32),
                   jax.ShapeDtypeStruct((TOP_K, n), F32),
                   jax.ShapeDtypeStruct((e, 1), F32)),
        grid=(n // t,),
        in_specs=[pl.BlockSpec((e, t), lambda i: (0, i)),
                  pl.BlockSpec((e, 1), lambda i: (0, 0))],
        out_specs=(row_spec, row_spec, row_spec, pl.BlockSpec((e, 1), lambda i: (0, 0))),
        compiler_params=_cparams(("arbitrary",)),
        name="route",
    )(logits_t, bias_col)


def _dest_kernel(cnt_ref, idx_ref, rank_ref, dest_ref, bs_ref, nb_ref, pstart_ref):
    e = cnt_ref.shape[0]

    @pl.when(pl.program_id(0) == 0)
    def _():
        nblk = jnp.floor((cnt_ref[...] + float(MOE_BLOCK - 1)) * (1.0 / MOE_BLOCK))
        hi = jnp.floor(nblk * (1.0 / 32.0))
        lo = nblk - 32.0 * hi
        r = lax.broadcasted_iota(I32, (e, e), 0)
        c = lax.broadcasted_iota(I32, (e, e), 1)
        below = jnp.where(c < r, 1.0, 0.0).astype(BF16)
        hi_b = jnp.broadcast_to(hi, (e, SLAB)).astype(BF16)
        lo_b = jnp.broadcast_to(lo, (e, SLAB)).astype(BF16)
        bstart = 32.0 * _dot(below, hi_b) + _dot(below, lo_b)
        pstart_ref[...] = bstart[:, 0:1] * float(MOE_BLOCK)
        bs_ref[...] = bstart[:, 0:1].astype(I32)
        nb_ref[...] = nblk.astype(I32)

    eidx = lax.broadcasted_iota(I32, (e, idx_ref.shape[1]), 0)
    for k in range(TOP_K):
        hit = eidx == idx_ref[k:k + 1, :]
        base = jnp.sum(jnp.where(hit, pstart_ref[...], 0.0), axis=0, keepdims=True)
        dest_ref[k:k + 1, :] = base.astype(I32) + rank_ref[k:k + 1, :]


def _dest(counts, idx_t, rank_t, *, t):
    e = counts.shape[0]
    n = idx_t.shape[1]
    row_spec = pl.BlockSpec((TOP_K, t), lambda i: (0, i))
    return pl.pallas_call(
        _dest_kernel,
        out_shape=(jax.ShapeDtypeStruct((TOP_K, n), I32),
                   jax.ShapeDtypeStruct((e, 1), I32),
                   jax.ShapeDtypeStruct((e, 1), I32)),
        grid=(n // t,),
        in_specs=[pl.BlockSpec((e, 1), lambda i: (0, 0)), row_spec, row_spec],
        out_specs=(row_spec,
                   pl.BlockSpec((e, 1), lambda i: (0, 0)),
                   pl.BlockSpec((e, 1), lambda i: (0, 0))),
        scratch_shapes=[pltpu.VMEM((e, 1), F32)],
        compiler_params=_cparams(("arbitrary",)),
        name="dest",
    )(counts, idx_t, rank_t)


SC_CORES = 2
SC_SUBCORES = 16
SC_CHUNK = 64


def _sc_scatter_kernel(up_ref, dest_ref, xs_ref, rows, idx, sem, *, t, tiles_per_worker):
    wid = lax.axis_index("s") * SC_CORES + lax.axis_index("c")

    def body(i, _):
        tile = wid * tiles_per_worker + i
        pltpu.sync_copy(up_ref.at[pl.ds(pl.multiple_of(tile * t, t), t)], rows)
        pltpu.sync_copy(dest_ref.at[pl.ds(pl.multiple_of(tile * TOP_K, TOP_K), TOP_K)], idx)
        copies = [pltpu.async_copy(rows, xs_ref.at[idx.at[k]], sem) for k in range(TOP_K)]
        for cp in copies:
            cp.wait()
        return 0

    lax.fori_loop(0, tiles_per_worker, body, 0)


def _sc_scatter(up, dest_flat, cap, *, t):
    n, width = up.shape
    workers = SC_CORES * SC_SUBCORES
    tiles_per_worker = n // t // workers
    assert tiles_per_worker * workers * t == n
    mesh = plsc.VectorSubcoreMesh(core_axis_name="c", subcore_axis_name="s")
    return pl.kernel(
        functools.partial(_sc_scatter_kernel, t=t, tiles_per_worker=tiles_per_worker),
        out_type=jax.ShapeDtypeStruct((cap, width), up.dtype),
        mesh=mesh,
        scratch_types=[pltpu.VMEM((t, width), up.dtype), pltpu.VMEM((TOP_K, t), I32), pltpu.SemaphoreType.DMA],
        name="sc_scatter",
    )(up, dest_flat.reshape(-1, t))


def _unpack(words):
    lo, hi = _unpack_f32(words)
    return lo.astype(BF16), hi.astype(BF16)


def _expert_kernel(bs_ref, nb_ref, cnt_ref, xs_ref, wg_ref, wu_ref, wd_ref, ys_ref,
                   wgu_s, wd_s, xbuf, ybuf, xsem, ysem, *, n_blocks, n_experts):
    e = pl.program_id(0)
    last = n_experts - 1
    d_half = xbuf.shape[2]
    d_exp = wg_ref.shape[2]
    nb = nb_ref[e]
    first = bs_ref[e]
    used = bs_ref[last] + nb_ref[last]

    def rows_of(blk):
        return pl.ds(pl.multiple_of(blk * MOE_BLOCK, MOE_BLOCK), MOE_BLOCK)

    def x_copy(blk, slot):
        return pltpu.make_async_copy(xs_ref.at[rows_of(blk)], xbuf.at[slot], xsem.at[slot])

    def y_copy(blk, slot):
        return pltpu.make_async_copy(ybuf.at[slot], ys_ref.at[rows_of(blk)], ysem.at[slot])

    @pl.when((e == 0) & (used > 0))
    def _():
        x_copy(0, 0).start()

    @pl.when(nb > 0)
    def _():
        wgu_s[:, :d_exp] = wg_ref[0].astype(BF16)
        wgu_s[:, d_exp:] = wu_ref[0].astype(BF16)
        wd_s[...] = wd_ref[0].astype(BF16)

        def block(g, _):
            slot = lax.rem(g, 2)
            x_copy(g, slot).wait()

            @pl.when(g + 1 < used)
            def _():
                x_copy(g + 1, 1 - slot).start()

            @pl.when(g >= 2)
            def _():
                y_copy(g - 2, slot).wait()

            rows = MOE_BLOCK // MOE_SPLIT
            for part in range(MOE_SPLIT):
                sl = slice(part * rows, (part + 1) * rows)
                row = lax.broadcasted_iota(I32, (rows, d_half), 0) + (part * rows)
                live = row < cnt_ref[e] - (g - first) * MOE_BLOCK
                lo, hi = _unpack(jnp.where(live, xbuf[slot, sl, :], jnp.uint32(0)))
                gu = _dot(lo, wgu_s[:d_half, :]) + _dot(hi, wgu_s[d_half:, :])
                gate = gu[:, :d_exp]
                h = gate * jax.nn.sigmoid(gate) * gu[:, d_exp:]
                ybuf[slot, sl, :] = _pack(_dot(h.astype(BF16), wd_s[...]))
            y_copy(g, slot).start()
            return 0

        lax.fori_loop(first, first + nb, block, 0)

    @pl.when(e == last)
    def _():
        @pl.when(used >= 2)
        def _():
            y_copy(used - 2, lax.rem(used, 2)).wait()

        @pl.when(used >= 1)
        def _():
            y_copy(used - 1, lax.rem(used - 1, 2)).wait()

        ybuf[0] = jnp.zeros(ybuf.shape[1:], ybuf.dtype)

        def fill(blk, _):
            y_copy(blk, 0).start()
            return 0

        def done(blk, _):
            y_copy(blk, 0).wait()
            return 0

        lax.fori_loop(used, n_blocks, fill, 0)
        lax.fori_loop(used, n_blocks, done, 0)


def _experts(blk_start, blk_count, counts, xs, wg, wu, wd):
    cap, d_half = xs.shape
    d = 2 * d_half
    n_exp, _, d_exp = wg.shape
    grid_spec = pltpu.PrefetchScalarGridSpec(
        num_scalar_prefetch=3,
        grid=(n_exp,),
        in_specs=[pl.BlockSpec(memory_space=pl.ANY),
                  pl.BlockSpec((1, d, d_exp), lambda i, bs, nb, cnt: (i, 0, 0)),
                  pl.BlockSpec((1, d, d_exp), lambda i, bs, nb, cnt: (i, 0, 0)),
                  pl.BlockSpec((1, d_exp, d), lambda i, bs, nb, cnt: (i, 0, 0))],
        out_specs=pl.BlockSpec(memory_space=pl.ANY),
        scratch_shapes=[pltpu.VMEM((d, 2 * d_exp), BF16), pltpu.VMEM((d_exp, d), BF16),
                        pltpu.VMEM((2, MOE_BLOCK, d_half), U32), pltpu.VMEM((2, MOE_BLOCK, d_half), U32),
                        pltpu.SemaphoreType.DMA((2,)), pltpu.SemaphoreType.DMA((2,))],
    )
    return pl.pallas_call(
        functools.partial(_expert_kernel, n_blocks=cap // MOE_BLOCK, n_experts=n_exp),
        out_shape=jax.ShapeDtypeStruct((cap, d_half), U32),
        grid_spec=grid_spec,
        compiler_params=_cparams(("arbitrary",)),
        name="experts",
    )(blk_start, blk_count, counts, xs, wg, wu, wd)


def _sc_gather_kernel(table_ref, idx_ref, out_ref, idx, rows0, rows1, sem0, sem1, *, per_worker):
    wid = lax.axis_index("s") * SC_CORES + lax.axis_index("c")
    base = wid * per_worker
    chunks = per_worker // SC_CHUNK
    pltpu.sync_copy(idx_ref.at[pl.ds(pl.multiple_of(wid * chunks, chunks), chunks)], idx)

    def body(i, _):
        off0 = pl.multiple_of(base + (2 * i) * SC_CHUNK, SC_CHUNK)
        off1 = pl.multiple_of(off0 + SC_CHUNK, SC_CHUNK)
        g0 = pltpu.async_copy(table_ref.at[idx.at[2 * i]], rows0, sem0)
        g1 = pltpu.async_copy(table_ref.at[idx.at[2 * i + 1]], rows1, sem1)
        g0.wait()
        pltpu.sync_copy(rows0, out_ref.at[pl.ds(off0, SC_CHUNK)])
        g1.wait()
        pltpu.sync_copy(rows1, out_ref.at[pl.ds(off1, SC_CHUNK)])
        return 0

    lax.fori_loop(0, per_worker // (2 * SC_CHUNK), body, 0)


def _sc_gather(table, idx):
    rows = idx.shape[0]
    width = table.shape[1]
    workers = SC_CORES * SC_SUBCORES
    per_worker = rows // workers
    assert per_worker * workers == rows and per_worker % (2 * SC_CHUNK) == 0
    mesh = plsc.VectorSubcoreMesh(core_axis_name="c", subcore_axis_name="s")
    return pl.kernel(
        functools.partial(_sc_gather_kernel, per_worker=per_worker),
        out_type=jax.ShapeDtypeStruct((rows, width), table.dtype),
        mesh=mesh,
        scratch_types=[pltpu.VMEM((per_worker // SC_CHUNK, SC_CHUNK), I32),
                       pltpu.VMEM((SC_CHUNK, width), table.dtype), pltpu.VMEM((SC_CHUNK, width), table.dtype),
                       pltpu.SemaphoreType.DMA, pltpu.SemaphoreType.DMA],
        name="sc_gather",
    )(table, idx.reshape(-1, SC_CHUNK))


def _combine_kernel(yg_ref, w_ref, x1_ref, gate_ref, sc_ref, sh_ref,
                    wgu_ref, wd_ref, g_ref, b_ref, o_ref, *, alpha):
    x1 = x1_ref[...]
    ub = (x1 * (1.0 + sc_ref[0]) + sh_ref[0]).astype(BF16)
    d_sh = wd_ref.shape[0]
    gu = _dot(ub, wgu_ref[...])
    gate = gu[:, :d_sh]
    h = gate * jax.nn.sigmoid(gate) * gu[:, d_sh:]
    ffn = _dot(h.astype(BF16), wd_ref[...])

    w = w_ref[...]
    d_half = yg_ref.shape[3]
    lo_sum = ffn[:, :d_half]
    hi_sum = ffn[:, d_half:]
    for k in range(TOP_K):
        lo, hi = _unpack_f32(yg_ref[0, k])
        lo_sum = lo_sum + w[:, k:k + 1] * lo
        hi_sum = hi_sum + w[:, k:k + 1] * hi
    ffn = jnp.concatenate([lo_sum, hi_sum], axis=1)
    o_ref[...] = _layer_norm(alpha * x1 + (1.0 + gate_ref[0]) * ffn, g_ref[...], b_ref[...])


def _combine(yg, w_tok, x1, gate, sc, sh, wgu, wd, g, bb, *, t, seq, alpha):
    n, d = x1.shape
    per_batch = seq // t
    mod_spec = pl.BlockSpec((1, 1, d), lambda i: (i // per_batch, 0, 0))
    return pl.pallas_call(
        functools.partial(_combine_kernel, alpha=alpha),
        out_shape=jax.ShapeDtypeStruct((n, d), F32),
        grid=(n // t,),
        in_specs=[pl.BlockSpec((1, TOP_K, t, d // 2), lambda i: (i, 0, 0, 0)),
                  pl.BlockSpec((t, TOP_K), lambda i: (i, 0)),
                  pl.BlockSpec((t, d), lambda i: (i, 0)),
                  mod_spec, mod_spec, mod_spec,
                  pl.BlockSpec(wgu.shape, lambda i: (0, 0)),
                  pl.BlockSpec(wd.shape, lambda i: (0, 0)),
                  pl.BlockSpec((1, d), lambda i: (0, 0)),
                  pl.BlockSpec((1, d), lambda i: (0, 0))],
        out_specs=pl.BlockSpec((t, d), lambda i: (i, 0)),
        compiler_params=_cparams(("arbitrary",)),
        name="combine",
    )(yg, w_tok, x1, gate, sc, sh, wgu, wd, g, bb)


def _slab_cols(w, scale=1.0):
    d = w.shape[0]
    w = (w * scale).reshape(d, N_HEADS, HEAD_DIM)
    w = jnp.pad(w, ((0, 0), (0, 0), (0, SLAB - HEAD_DIM)))
    return w.reshape(d, N_HEADS * SLAB)


def _placement():
    h = jnp.arange(SLAB)[:, None]
    c = jnp.arange(N_HEADS * SLAB)[None, :]
    mats = [((h < N_HEADS) & (c == h * SLAB + HEAD_DIM + p)) for p in range(3)]
    return jnp.stack(mats).astype(BF16)


def _tile_major(rows, t):
    k, n = rows.shape
    return rows.reshape(k, n // t, t).transpose(1, 0, 2).reshape(-1)


def _layer(x, c_pad, w_ada, b_ada, w_in, b_f, fox_g, sb_g, w_out, ln1_g, ln1_b,
           w_router, router_bias, w_gate_e, w_up_e, w_down_e,
           w_gate_sh, w_up_sh, w_down_sh, ln2_g, ln2_b, *, alpha,
           t_proj=512, t_attn=512, attn_heads=8, t_route=512, t_moe=128):
    b, s, d = x.shape
    n = b * s
    d_val = N_HEADS * HEAD_DIM
    scale = HEAD_DIM ** -0.5 * LOG2_E

    ada = _ada(c_pad, w_ada, b_ada.reshape(1, -1))[:b]
    shift1, scale1, gate1, shift2, scale2, gate2 = [m[:, None, :] for m in jnp.split(ada, 6, axis=-1)]

    q_f, k_f, v_f, q_s, k_s, v_s, w_f = jnp.split(
        w_in, [d_val, 2 * d_val, 3 * d_val, 4 * d_val, 5 * d_val, 6 * d_val], axis=1)
    w_f = jnp.pad(w_f, ((0, 0), (0, SLAB - N_HEADS)))
    wtok = jnp.concatenate([_slab_cols(k_f), _slab_cols(k_s), w_f], axis=1).astype(BF16)
    wfeat = jnp.concatenate([_slab_cols(q_f, scale), _slab_cols(q_s, scale), v_f, v_s], axis=1).T.astype(BF16)
    bf_row = jnp.pad(b_f, (0, SLAB - N_HEADS)).reshape(1, SLAB)

    kf, ks, qf_t, qs_t, vf_t, vs_t = _inproj(x, scale1, shift1, wtok, wfeat, bf_row, _placement(),
                                             t=t_proj, tk=t_attn)
    of_t = _attention(_fox_kernel, qf_t, kf, vf_t, fox_g.reshape(-1, 1),
                      tq=t_attn, heads=attn_heads, name="fox")
    os_t = _attention(_sb_kernel, qs_t, ks, vs_t, sb_g.reshape(-1, 1),
                      tq=t_attn, heads=attn_heads, name="sb")

    x1, u_packed, logits_t = _outproj(
        of_t, os_t, x, gate1, scale2, shift2,
        w_out[:d_val].astype(BF16), w_out[d_val:].astype(BF16),
        ln1_g.reshape(1, d), ln1_b.reshape(1, d), w_router.T, t=t_proj, alpha=alpha)

    idx_t, rank_t, w_t, counts = _route(logits_t, router_bias.reshape(-1, 1), t=t_route)
    n_blocks = -(-(n * TOP_K) // MOE_BLOCK) + N_EXPERTS
    dest_t, blk_start, blk_count = _dest(counts, idx_t, rank_t, t=t_route)
    dest_flat = _tile_major(dest_t, t_moe)

    xs = _sc_scatter(u_packed, dest_flat, n_blocks * MOE_BLOCK, t=t_moe)
    ys = _experts(blk_start.reshape(-1), blk_count.reshape(-1), counts.astype(I32).reshape(-1),
                  xs, w_gate_e, w_up_e, w_down_e)

    wgu_sh = jnp.concatenate([w_gate_sh, w_up_sh], axis=1).astype(BF16)
    yg = _sc_gather(ys, dest_flat).reshape(n // t_moe, TOP_K, t_moe, d // 2)
    out = _combine(yg, w_t.T, x1, gate2, scale2, shift2, wgu_sh, w_down_sh.astype(BF16),
                   ln2_g.reshape(1, d), ln2_b.reshape(1, d), t=t_moe, seq=s, alpha=alpha)
    return out.reshape(b, s, d)


def kernel(x, c, w_ada, b_ada, w_in, b_f, fox_norm_g, sb_norm_g, w_out, ln1_g, ln1_b, w_router, router_bias, w_gate_e, w_up_e, w_down_e, w_gate_sh, w_up_sh, w_down_sh, ln2_g, ln2_b):
    depth = w_ada.shape[0]
    alpha = (2.0 * depth) ** 0.25
    c_pad = jnp.pad(c, ((0, (-c.shape[0]) % 8), (0, 0)))
    for l in range(depth):
        x = _layer(x, c_pad, w_ada[l], b_ada[l], w_in[l], b_f[l], fox_norm_g[l], sb_norm_g[l],
                   w_out[l], ln1_g[l], ln1_b[l], w_router[l], router_bias[l],
                   w_gate_e[l], w_up_e[l], w_down_e[l], w_gate_sh[l], w_up_sh[l], w_down_sh[l],
                   ln2_g[l], ln2_b[l], alpha=alpha)
    return x
```
